```python
import math
import jax, jax.numpy as jnp
from jax import lax
import numpy as np

D_MODEL = 1024
BATCH = 16
SEQ = 4096
DEPTH = 1
DEC_BATCH = 1
DEC_SEQ = 16384
PAST_LEN = 128

GRID_W = 64
D_MIX = D_MODEL
NA_HEAD_DIM = 64
NA_HEADS = (D_MIX // 2) // NA_HEAD_DIM
NA_WIDTH = NA_HEADS * NA_HEAD_DIM
NA_WIN_H = 8
NA_WIN_W = 16
NA_QBLK = 16
NA_KBLK = 32
RET_HEAD_DIM = 128
RET_HEADS = (D_MIX // 2) // RET_HEAD_DIM
RET_WIDTH = RET_HEADS * RET_HEAD_DIM
RET_CHUNK = 128
ROPE_BASE = 10000.0
NORM_EPS = 1e-6
IN_WIDTH = 4 * NA_WIDTH + 4 * RET_WIDTH

kernel_name = 'hybrid_na_retention_encoder'


def rms_norm(x, gain):
    xf = x.astype(jnp.float32)
    y = xf * lax.rsqrt(jnp.mean(xf * xf, axis=-1, keepdims=True) + NORM_EPS)
    return (y * gain.astype(jnp.float32)).astype(x.dtype)


def rotary(x):
    T, d = x.shape[2], x.shape[3]
    half = d // 2
    inv = ROPE_BASE ** (-jnp.arange(half, dtype=jnp.float32) / half)
    ang = jnp.arange(T, dtype=jnp.float32)[:, None] * inv[None, :]
    cos, sin = jnp.cos(ang), jnp.sin(ang)
    x1 = x[..., :half].astype(jnp.float32)
    x2 = x[..., half:].astype(jnp.float32)
    return jnp.concatenate([x1 * cos - x2 * sin, x1 * sin + x2 * cos], axis=-1).astype(x.dtype)


def _na_column_blocks():
    q_col = np.arange(GRID_W)
    win_start = np.clip(q_col - NA_WIN_W // 2, 0, GRID_W - NA_WIN_W)
    n_blk = GRID_W // NA_QBLK
    blk = np.arange(n_blk)
    k_start = np.clip(blk * NA_QBLK - NA_WIN_W // 2, 0, GRID_W - NA_KBLK)
    k_idx = k_start[:, None] + np.arange(NA_KBLK)
    q_idx = blk[:, None] * NA_QBLK + np.arange(NA_QBLK)
    ws = win_start[q_idx][:, :, None]
    kk = k_idx[:, None, :]
    valid = (kk >= ws) & (kk < ws + NA_WIN_W)
    dc = np.clip(kk - q_idx[:, :, None] + NA_WIN_W - 1, 0, 2 * NA_WIN_W - 2)
    return k_idx, valid, dc


def neighborhood_attention(q, k, v, rpb):
    B, H, T, d = q.shape
    rows = T // GRID_W
    kh = min(NA_WIN_H, rows)
    k_idx, valid, dc = _na_column_blocks()
    n_blk = k_idx.shape[0]
    kg = k.reshape(B, H, rows, GRID_W, d)
    vg = v.reshape(B, H, rows, GRID_W, d)
    qg = jnp.moveaxis(q.reshape(B, H, rows, GRID_W, d), 2, 0)
    bias_c = rpb.astype(jnp.float32)[:, :, dc]
    bias_c = jnp.where(valid, bias_c, -jnp.inf)

    def row_step(args):
        r, q_r = args
        start = jnp.clip(r - kh // 2, 0, rows - kh)
        k_rows = lax.dynamic_slice_in_dim(kg, start, kh, axis=2)
        v_rows = lax.dynamic_slice_in_dim(vg, start, kh, axis=2)
        k_blk = k_rows[:, :, :, k_idx]
        v_blk = v_rows[:, :, :, k_idx]
        q_b = q_r.reshape(B, H, n_blk, NA_QBLK, d)
        s = jnp.einsum('bhnqd,bhrnkd->bhnqrk', q_b, k_blk).astype(jnp.float32)
        dr = start + jnp.arange(kh) - r + (NA_WIN_H - 1)
        bias = jnp.transpose(bias_c[:, dr], (0, 2, 3, 1, 4))
        s = s + bias[None]
        p = jax.nn.softmax(s.reshape(B, H, n_blk, NA_QBLK, kh * NA_KBLK), axis=-1).reshape(s.shape)
        o = jnp.einsum('bhnqrk,bhrnkd->bhnqd', p.astype(v.dtype), v_blk)
        return o.reshape(B, H, GRID_W, d)

    out = lax.map(row_step, (jnp.arange(rows), qg))
    return jnp.moveaxis(out, 0, 2).reshape(B, H, T, d)


def retention_chunkwise(q, k, v, log_gamma, strict):
    B, H, T, dk = q.shape
    dv = v.shape[-1]
    C = RET_CHUNK
    N = T // C
    q = q.reshape(B, H, N, C, dk)
    k = k.reshape(B, H, N, C, dk)
    v = v.reshape(B, H, N, C, dv)
    pos = jnp.arange(C, dtype=jnp.float32)
    diff = pos[:, None] - pos[None, :]
    lower = (diff > 0) if strict else (diff >= 0)
    dmat = jnp.where(lower, jnp.exp(log_gamma[:, None, None] * jnp.maximum(diff, 0.0)), 0.0)
    s = jnp.einsum('bhnid,bhnjd->bhnij', q, k) * dmat[None, :, None]
    o_intra = jnp.einsum('bhnij,bhnjd->bhnid', s, v)
    k_dec = k * jnp.exp(log_gamma[:, None] * (C - 1 - pos)[None, :])[None, :, None, :, None]
    kv = jnp.einsum('bhnjd,bhnje->nbhde', k_dec, v).astype(jnp.float32)
    chunk_decay = jnp.exp(log_gamma * C)[None, :, None, None]

    def step(S, kv_n):
        return chunk_decay * S + kv_n, S

    _, S_prev = lax.scan(step, jnp.zeros((B, H, dk, dv), jnp.float32), kv)
    q_dec = q * jnp.exp(log_gamma[:, None] * (pos + 1.0)[None, :])[None, :, None, :, None]
    o_cross = jnp.einsum('bhnid,nbhde->bhnie', q_dec, S_prev)
    return (o_intra + o_cross).reshape(B, H, T, dv)


def encoder_layer(x, c, norm_gain, w_ada, b_ada, w_in, na_q_gain, na_k_gain, na_rpb,
                  ret_decay_f, ret_decay_b, ret_out_gain, w_out):
    B, T, _ = x.shape
    mod = jax.nn.silu(c) @ w_ada + b_ada
    shift, scale, gate = jnp.split(mod, 3, axis=-1)
    h = rms_norm(x, norm_gain) * (1 + scale[:, None]) + shift[:, None]
    proj = h @ w_in
    sizes = [NA_WIDTH] * 4 + [RET_WIDTH] * 4
    qa, ka, va, ga, qr, kr, vr, gr = jnp.split(proj, list(np.cumsum(sizes)[:-1]), axis=-1)

    def to_heads(t, n):
        return t.reshape(B, T, n, -1).transpose(0, 2, 1, 3)

    qa = rms_norm(to_heads(qa, NA_HEADS), na_q_gain) * (NA_HEAD_DIM ** -0.5)
    ka = rms_norm(to_heads(ka, NA_HEADS), na_k_gain)
    na = neighborhood_attention(qa, ka, to_heads(va, NA_HEADS), na_rpb)
    na = na.transpose(0, 2, 1, 3).reshape(B, T, NA_WIDTH) * jax.nn.silu(ga)

    qr = rotary(to_heads(qr, RET_HEADS))
    kr = rotary(to_heads(kr, RET_HEADS)) * (RET_HEAD_DIM ** -0.5)
    vr = to_heads(vr, RET_HEADS)
    lg_f = -jnp.exp(ret_decay_f.astype(jnp.float32))
    lg_b = -jnp.exp(ret_decay_b.astype(jnp.float32))
    flip = lambda t: t[:, :, ::-1]
    fwd = retention_chunkwise(qr, kr, vr, lg_f, strict=False)
    bwd = flip(retention_chunkwise(flip(qr), flip(kr), flip(vr), lg_b, strict=True))
    ret = (fwd + bwd).transpose(0, 2, 1, 3)
    ret = rms_norm(ret, ret_out_gain).reshape(B, T, RET_WIDTH).astype(x.dtype) * jax.nn.silu(gr)

    mix = jnp.concatenate([na, ret], axis=-1) @ w_out
    return x + gate[:, None] * mix


def setup_inputs(seed: int = 0) -> dict:
    key = jax.random.key(seed)
    ks = jax.random.split(key, 16)
    f32 = jnp.float32
    base = jnp.log(-jnp.log1p(-(2.0 ** (-5.0 - jnp.arange(RET_HEADS, dtype=f32)))))
    return {
        'x_prompt': jax.random.normal(ks[0], (BATCH, SEQ, D_MODEL), f32),
        'x_sample': jax.random.normal(ks[1], (DEC_BATCH, DEC_SEQ, D_MODEL), f32),
        'c_prompt': jax.random.normal(ks[2], (BATCH, D_MODEL), f32),
        'c_sample': jax.random.normal(ks[3], (DEC_BATCH, D_MODEL), f32),
        'norm_gain': 1.0 + 0.02 * jax.random.normal(ks[4], (DEPTH, D_MODEL), f32),
        'w_ada': 0.5 * D_MODEL ** -0.5 * jax.random.normal(ks[5], (DEPTH, D_MODEL, 3 * D_MODEL), f32),
        'b_ada': 0.02 * jax.random.normal(ks[6], (DEPTH, 3 * D_MODEL), f32),
        'w_in': D_MODEL ** -0.5 * jax.random.normal(ks[7], (DEPTH, D_MODEL, IN_WIDTH), f32),
        'na_q_gain': 1.0 + 0.02 * jax.random.normal(ks[8], (DEPTH, NA_HEAD_DIM), f32),
        'na_k_gain': 1.0 + 0.02 * jax.random.normal(ks[9], (DEPTH, NA_HEAD_DIM), f32),
        'na_rpb': 0.02 * jax.random.normal(ks[10], (DEPTH, NA_HEADS, 2 * NA_WIN_H - 1, 2 * NA_WIN_W - 1), f32),
        'ret_decay_f': base[None] + 0.01 * jax.random.normal(ks[11], (DEPTH, RET_HEADS), f32),
        'ret_decay_b': base[None] + 0.01 * jax.random.normal(ks[12], (DEPTH, RET_HEADS), f32),
        'ret_out_gain': 1.0 + 0.02 * jax.random.normal(ks[13], (DEPTH, RET_HEADS, RET_HEAD_DIM), f32),
        'w_out': D_MIX ** -0.5 * jax.random.normal(ks[14], (DEPTH, D_MIX, D_MODEL), f32),
    }


def reference(x_prompt, x_sample, c_prompt, c_sample, norm_gain, w_ada, b_ada, w_in,
              na_q_gain, na_k_gain, na_rpb, ret_decay_f, ret_decay_b, ret_out_gain, w_out):
    y_prompt = x_prompt
    y_sample = x_sample
    for l in range(DEPTH):
        y_prompt = encoder_layer(y_prompt, c_prompt, norm_gain[l], w_ada[l], b_ada[l], w_in[l],
                                 na_q_gain[l], na_k_gain[l], na_rpb[l], ret_decay_f[l],
                                 ret_decay_b[l], ret_out_gain[l], w_out[l])
        y_sample = encoder_layer(y_sample, c_sample, norm_gain[l], w_ada[l], b_ada[l], w_in[l],
                                 na_q_gain[l], na_k_gain[l], na_rpb[l], ret_decay_f[l],
                                 ret_decay_b[l], ret_out_gain[l], w_out[l])
    return (y_prompt, y_sample)
```

```python
import functools

import numpy as np
import jax
import jax.numpy as jnp
from jax import lax
from jax.experimental import pallas as pl
from jax.experimental.pallas import tpu as pltpu

D_MODEL = 1024
GRID_W = 64
NA_HEADS = 8
NA_HEAD_DIM = 64
NA_WIDTH = NA_HEADS * NA_HEAD_DIM
NA_PAIRS = NA_HEADS // 2
NA_WIN_H = 8
NA_WIN_W = 16
RET_HEADS = 4
RET_HEAD_DIM = 128
RET_WIDTH = RET_HEADS * RET_HEAD_DIM
RET_CHUNK = 128
ROPE_BASE = 10000.0
NORM_EPS = 1e-6
IN_WIDTH = 4 * NA_WIDTH + 4 * RET_WIDTH
N_SEG = IN_WIDTH // 512

TOKEN_TILE = 512
HALO_ROWS = NA_WIN_H // 2
HALO_TOKENS = HALO_ROWS * GRID_W
BIAS_SLOTS = 2 * NA_WIN_H - 2
VMEM_LIMIT_BYTES = 56 * 1024 * 1024

_NT = (((1,), (1,)), ((), ()))
_TN = (((0,), (0,)), ((), ()))


def _silu(v):
    return v / (1.0 + jnp.exp(-v))


def _prep_kernel(dec_f_ref, dec_b_ref, c_ref, w_ref, b_ref,
                 mod_ref, dmat_ref, qf_ref, qb_ref, kf_ref, kb_ref, cdec_ref):
    c = c_ref[...]
    mod_ref[...] = jnp.dot(_silu(c), w_ref[...], preferred_element_type=jnp.float32) + b_ref[...]

    @pl.when(pl.program_id(0) == 0)
    def _():
        C = RET_CHUNK
        head = lax.broadcasted_iota(jnp.int32, (1, RET_WIDTH), 1) // RET_HEAD_DIM
        df = jnp.zeros((1, RET_WIDTH), jnp.float32)
        db = jnp.zeros((1, RET_WIDTH), jnp.float32)
        for h in range(RET_HEADS):
            df = jnp.where(head == h, dec_f_ref[h], df)
            db = jnp.where(head == h, dec_b_ref[h], db)
        lgf = -jnp.exp(df)
        lgb = -jnp.exp(db)
        pos = lax.broadcasted_iota(jnp.int32, (C, RET_WIDTH), 0).astype(jnp.float32)
        qf_ref[...] = jnp.exp(lgf * (pos + 1.0))
        qb_ref[...] = jnp.exp(lgb * (C - pos))
        kf_ref[...] = jnp.exp(lgf * (C - 1.0 - pos))
        kb_ref[...] = jnp.exp(lgb * pos)
        row = lax.broadcasted_iota(jnp.int32, (8, RET_WIDTH), 0)
        cdec_ref[...] = jnp.where(row == 0, jnp.exp(lgf * C), jnp.exp(lgb * C))
        ri = lax.broadcasted_iota(jnp.int32, (C, C), 0)
        ci = lax.broadcasted_iota(jnp.int32, (C, C), 1)
        diff = (ri - ci).astype(jnp.float32)
        for h in range(RET_HEADS):
            lf = lgf[:, h * RET_HEAD_DIM:(h + 1) * RET_HEAD_DIM]
            lb = lgb[:, h * RET_HEAD_DIM:(h + 1) * RET_HEAD_DIM]
            dmat_ref[h] = jnp.where(diff >= 0, jnp.exp(lf * jnp.maximum(diff, 0.0)),
                                    jnp.exp(lb * jnp.maximum(-diff, 0.0)))


def _prep(c_all, w_ada, b_ada, dec_f, dec_b):
    nb = c_all.shape[0]
    C = RET_CHUNK
    f32 = jnp.float32
    smem = pl.BlockSpec(memory_space=pltpu.SMEM)
    const2 = lambda j: (0, 0)
    return pl.pallas_call(
        _prep_kernel,
        grid=(3,),
        in_specs=[smem, smem,
                  pl.BlockSpec((nb, D_MODEL), const2),
                  pl.BlockSpec((D_MODEL, D_MODEL), lambda j: (0, j)),
                  pl.BlockSpec((1, D_MODEL), lambda j: (0, j))],
        out_specs=[pl.BlockSpec((nb, D_MODEL), lambda j: (0, j)),
                   pl.BlockSpec((RET_HEADS, C, C), lambda j: (0, 0, 0)),
                   pl.BlockSpec((C, RET_WIDTH), const2),
                   pl.BlockSpec((C, RET_WIDTH), const2),
                   pl.BlockSpec((C, RET_WIDTH), const2),
                   pl.BlockSpec((C, RET_WIDTH), const2),
                   pl.BlockSpec((8, RET_WIDTH), const2)],
        out_shape=[jax.ShapeDtypeStruct((nb, 3 * D_MODEL), f32),
                   jax.ShapeDtypeStruct((RET_HEADS, C, C), f32),
                   jax.ShapeDtypeStruct((C, RET_WIDTH), f32),
                   jax.ShapeDtypeStruct((C, RET_WIDTH), f32),
                   jax.ShapeDtypeStruct((C, RET_WIDTH), f32),
                   jax.ShapeDtypeStruct((C, RET_WIDTH), f32),
                   jax.ShapeDtypeStruct((8, RET_WIDTH), f32)],
        compiler_params=pltpu.CompilerParams(dimension_semantics=("arbitrary",),
                                             vmem_limit_bytes=VMEM_LIMIT_BYTES),
        name="prep",
    )(dec_f, dec_b, c_all, w_ada, b_ada)


def _in_proj_kernel(x_ref, mod_ref, gain_ref, w_ref, bd_ref, qg_ref, kg_ref, cos_ref, sin_ref,
                    kb_ref, cdec_ref,
                    qa_ref, ka_ref, va_ref, ga_ref, qr_ref, kr_ref, vr_ref, gr_ref, sb_ref,
                    state_ref):
    bf16, f32 = jnp.bfloat16, jnp.float32
    tt = x_ref.shape[0]

    @pl.when(pl.program_id(1) == 0)
    def _():
        state_ref[...] = jnp.zeros_like(state_ref)

    x = x_ref[...]
    ms = jnp.mean(x * x, axis=-1, keepdims=True)
    a = gain_ref[...] * (1.0 + mod_ref[1:2, :])
    hb = (x * lax.rsqrt(ms + NORM_EPS) * a + mod_ref[0:1, :]).astype(bf16)

    def seg(s):
        return jnp.dot(hb, w_ref[:, s * 512:(s + 1) * 512], preferred_element_type=f32)

    def head_norm(p, g):
        ss = jnp.dot((p * p).astype(bf16), bd_ref[...], preferred_element_type=f32)
        return p * lax.rsqrt(ss * (1.0 / NA_HEAD_DIM) + NORM_EPS) * g

    qa_ref[...] = head_norm(seg(0), qg_ref[...] * (NA_HEAD_DIM ** -0.5)).astype(bf16)
    ka_ref[...] = head_norm(seg(1), kg_ref[...]).astype(bf16)
    va_ref[...] = seg(2).astype(bf16)
    ga_ref[...] = _silu(seg(3)).astype(bf16)

    cosf = cos_ref[...]
    sinf = sin_ref[...]

    def rotary(p):
        outs = []
        for h in range(RET_HEADS):
            ph = p[:, h * RET_HEAD_DIM:(h + 1) * RET_HEAD_DIM]
            outs.append(ph * cosf + pltpu.roll(ph, RET_HEAD_DIM // 2, 1) * sinf)
        return jnp.concatenate(outs, axis=1)

    qr_ref[...] = rotary(seg(4)).astype(bf16)
    kr = rotary(seg(5)) * (RET_HEAD_DIM ** -0.5)
    kr_ref[...] = kr.astype(bf16)
    vb = seg(6).astype(bf16)
    vr_ref[...] = vb
    gr_ref[...] = _silu(seg(7)).astype(bf16)

    kb = kb_ref[...]
    C = RET_CHUNK
    for c in reversed(range(tt // C)):
        kd = (kr[c * C:(c + 1) * C, :] * kb).astype(bf16)
        for h in range(RET_HEADS):
            hs = slice(h * RET_HEAD_DIM, (h + 1) * RET_HEAD_DIM)
            s_old = state_ref[h]
            sb_ref[c, h] = s_old.astype(bf16)
            kv = lax.dot_general(kd[:, hs], vb[c * C:(c + 1) * C, hs], _TN,
                                 preferred_element_type=f32)
            state_ref[h] = s_old * cdec_ref[1:2, hs] + kv


def _in_proj(x, mod3, gain, w_in, bd, qg, kg, cosf, sinf, kb, cdec):
    B, T, _ = x.shape
    tt = TOKEN_TILE
    nt = T // tt
    C = RET_CHUNK
    rev = lambda b, i: (b, nt - 1 - i, 0)
    c2 = lambda b, i: (0, 0)
    tok_spec = pl.BlockSpec((None, tt, 512), rev)
    out_tok = jax.ShapeDtypeStruct((B, T, 512), jnp.bfloat16)
    return pl.pallas_call(
        _in_proj_kernel,
        grid=(B, nt),
        in_specs=[pl.BlockSpec((None, tt, D_MODEL), rev),
                  pl.BlockSpec((None, 3, D_MODEL), lambda b, i: (b, 0, 0)),
                  pl.BlockSpec((1, D_MODEL), c2),
                  pl.BlockSpec((D_MODEL, IN_WIDTH), c2),
                  pl.BlockSpec((512, 512), c2),
                  pl.BlockSpec((1, 512), c2),
                  pl.BlockSpec((1, 512), c2),
                  pl.BlockSpec((tt, RET_HEAD_DIM), lambda b, i: (nt - 1 - i, 0)),
                  pl.BlockSpec((tt, RET_HEAD_DIM), lambda b, i: (nt - 1 - i, 0)),
                  pl.BlockSpec((C, RET_WIDTH), c2),
                  pl.BlockSpec((8, RET_WIDTH), c2)],
        out_specs=[tok_spec] * 8 + [
            pl.BlockSpec((None, tt // C, RET_HEADS, C, C), lambda b, i: (b, nt - 1 - i, 0, 0, 0))],
        out_shape=[out_tok] * 8 + [
            jax.ShapeDtypeStruct((B, T // C, RET_HEADS, C, C), jnp.bfloat16)],
        scratch_shapes=[pltpu.VMEM((RET_HEADS, C, C), jnp.float32)],
        compiler_params=pltpu.CompilerParams(dimension_semantics=("arbitrary", "arbitrary"),
                                             vmem_limit_bytes=VMEM_LIMIT_BYTES),
        name="in_proj",
    )(x, mod3, gain, w_in, bd, qg, kg, cosf, sinf, kb, cdec)


def _mix_out_kernel(x_ref, mod_ref, qa_ref, ga_ref, kap_ref, ka_ref, kan_ref,
                    vap_ref, va_ref, van_ref, bias_ref,
                    qr_ref, kr_ref, vr_ref, gr_ref, sb_ref,
                    dmat_ref, qf_ref, qb_ref, kf_ref, cdec_ref, og_ref, wo_ref,
                    y_ref,
                    state_ref, kbuf_ref, vbuf_ref, mix_ref, *, rows):
    bf16, f32 = jnp.bfloat16, jnp.float32
    tt = x_ref.shape[0]
    n_rows = tt // GRID_W
    i = pl.program_id(1)

    @pl.when(i == 0)
    def _():
        state_ref[...] = jnp.zeros_like(state_ref)

    kbuf_ref[0:HALO_TOKENS, :] = kap_ref[...]
    kbuf_ref[HALO_TOKENS:HALO_TOKENS + tt, :] = ka_ref[...]
    kbuf_ref[HALO_TOKENS + tt:, :] = kan_ref[...]
    vbuf_ref[0:HALO_TOKENS, :] = vap_ref[...]
    vbuf_ref[HALO_TOKENS:HALO_TOKENS + tt, :] = va_ref[...]
    vbuf_ref[HALO_TOKENS + tt:, :] = van_ref[...]

    lane = lax.broadcasted_iota(jnp.int32, (GRID_W, 2 * NA_HEAD_DIM), 1)
    low = lane < NA_HEAD_DIM
    win = NA_WIN_H * GRID_W

    def na_row(r, carry):
        grow = i * n_rows + r
        start = jnp.clip(grow - NA_WIN_H // 2, 0, rows - NA_WIN_H)
        off = grow - start
        w0 = pl.multiple_of((start - (i * n_rows - HALO_ROWS)) * GRID_W, GRID_W)
        q0 = pl.multiple_of(r * GRID_W, GRID_W)
        for p in range(NA_PAIRS):
            ls = slice(p * 128, (p + 1) * 128)
            q = qa_ref[pl.ds(q0, GRID_W), ls]
            q2 = jnp.concatenate([jnp.where(low, q, jnp.zeros_like(q)),
                                  jnp.where(low, jnp.zeros_like(q), q)], axis=0)
            kw = kbuf_ref[pl.ds(w0, win), ls]
            s = lax.dot_general(q2, kw, _NT, preferred_element_type=f32)
            s = jnp.concatenate(
                [s[:, j * 128:(j + 1) * 128] + bias_ref[p, 2 * j - off + (NA_WIN_H - 1)]
                 for j in range(win // 128)], axis=1)
            m = jnp.max(s, axis=-1, keepdims=True)
            e = jnp.exp(s - m)
            l = jnp.sum(e, axis=-1, keepdims=True)
            vw = vbuf_ref[pl.ds(w0, win), ls]
            o2 = jnp.dot(e.astype(bf16), vw, preferred_element_type=f32) * (1.0 / l)
            o = jnp.where(low, o2[0:GRID_W], o2[GRID_W:])
            g = ga_ref[pl.ds(q0, GRID_W), ls].astype(f32)
            mix_ref[pl.ds(q0, GRID_W), ls] = (o * g).astype(bf16)
        return carry

    lax.fori_loop(0, n_rows, na_row, 0)

    C = RET_CHUNK

    def ret_chunk(c, carry):
        t0 = pl.multiple_of(c * C, C)
        for h in range(RET_HEADS):
            hs = slice(h * RET_HEAD_DIM, (h + 1) * RET_HEAD_DIM)
            q = qr_ref[pl.ds(t0, C), hs]
            k = kr_ref[pl.ds(t0, C), hs]
            v = vr_ref[pl.ds(t0, C), hs]
            s = lax.dot_general(q, k, _NT, preferred_element_type=f32) * dmat_ref[h]
            o = jnp.dot(s.astype(bf16), v, preferred_element_type=f32)
            sf = state_ref[h]
            o = o + jnp.dot(q, sf.astype(bf16), preferred_element_type=f32) * qf_ref[:, hs]
            o = o + jnp.dot(q, sb_ref[c, h], preferred_element_type=f32) * qb_ref[:, hs]
            kd = (k.astype(f32) * kf_ref[:, hs]).astype(bf16)
            state_ref[h] = sf * cdec_ref[0:1, hs] + lax.dot_general(
                kd, v, _TN, preferred_element_type=f32)
            ms = jnp.mean(o * o, axis=-1, keepdims=True)
            rn = o * lax.rsqrt(ms + NORM_EPS) * og_ref[:, hs]
            g = gr_ref[pl.ds(t0, C), hs].astype(f32)
            mix_ref[pl.ds(t0, C), NA_WIDTH + h * RET_HEAD_DIM:NA_WIDTH + (h + 1) * RET_HEAD_DIM] = (
                rn * g).astype(bf16)
        return carry

    lax.fori_loop(0, tt // C, ret_chunk, 0)

    out = jnp.dot(mix_ref[...], wo_ref[...], preferred_element_type=f32)
    y_ref[...] = x_ref[...] + mod_ref[2:3, :] * out


def _mix_out(x, mod3, qa, ka, va, ga, qr, kr, vr, gr, sb, bias, dmat, qf, qb, kf, cdec, og, w_out):
    B, T, _ = x.shape
    tt = TOKEN_TILE
    nt = T // tt
    C = RET_CHUNK
    rows = T // GRID_W
    hb = tt // HALO_TOKENS
    n_halo = T // HALO_TOKENS
    tok = lambda b, i: (b, i, 0)
    prev = lambda b, i: (b, jnp.maximum(i * hb - 1, 0), 0)
    nxt = lambda b, i: (b, jnp.minimum((i + 1) * hb, n_halo - 1), 0)
    c2 = lambda b, i: (0, 0)
    tok_spec = pl.BlockSpec((None, tt, 512), tok)
    prev_spec = pl.BlockSpec((None, HALO_TOKENS, 512), prev)
    next_spec = pl.BlockSpec((None, HALO_TOKENS, 512), nxt)
    return pl.pallas_call(
        functools.partial(_mix_out_kernel, rows=rows),
        grid=(B, nt),
        in_specs=[pl.BlockSpec((None, tt, D_MODEL), tok),
                  pl.BlockSpec((None, 3, D_MODEL), lambda b, i: (b, 0, 0)),
                  tok_spec, tok_spec,
                  prev_spec, tok_spec, next_spec,
                  prev_spec, tok_spec, next_spec,
                  pl.BlockSpec((NA_PAIRS, BIAS_SLOTS, 128, 128), lambda b, i: (0, 0, 0, 0)),
                  tok_spec, tok_spec, tok_spec, tok_spec,
                  pl.BlockSpec((None, tt // C, RET_HEADS, C, C), lambda b, i: (b, i, 0, 0, 0)),
                  pl.BlockSpec((RET_HEADS, C, C), lambda b, i: (0, 0, 0)),
                  pl.BlockSpec((C, RET_WIDTH), c2),
                  pl.BlockSpec((C, RET_WIDTH), c2),
                  pl.BlockSpec((C, RET_WIDTH), c2),
                  pl.BlockSpec((8, RET_WIDTH), c2),
                  pl.BlockSpec((1, RET_WIDTH), c2),
                  pl.BlockSpec((D_MODEL, D_MODEL), c2)],
        out_specs=pl.BlockSpec((None, tt, D_MODEL), tok),
        out_shape=jax.ShapeDtypeStruct((B, T, D_MODEL), jnp.float32),
        scratch_shapes=[pltpu.VMEM((RET_HEADS, C, C), jnp.float32),
                        pltpu.VMEM((tt + 2 * HALO_TOKENS, 512), jnp.bfloat16),
                        pltpu.VMEM((tt + 2 * HALO_TOKENS, 512), jnp.bfloat16),
                        pltpu.VMEM((tt, D_MODEL), jnp.bfloat16)],
        compiler_params=pltpu.CompilerParams(dimension_semantics=("arbitrary", "arbitrary"),
                                             vmem_limit_bytes=VMEM_LIMIT_BYTES),
        name="mix_out",
    )(x, mod3, qa, ga, ka, ka, ka, va, va, va, bias, qr, kr, vr, gr, sb,
      dmat, qf, qb, kf, cdec, og, w_out)


def _bias_indices():
    m = np.arange(128)
    l = np.arange(128)
    q = (m % GRID_W)[:, None]
    kc = (l % GRID_W)[None, :]
    ws = np.clip(q - NA_WIN_W // 2, 0, GRID_W - NA_WIN_W)
    valid = (kc >= ws) & (kc < ws + NA_WIN_W)
    dc = np.clip(kc - q + NA_WIN_W - 1, 0, 2 * NA_WIN_W - 2)
    head = 2 * np.arange(NA_PAIRS)[:, None, None, None] + (m // GRID_W)[None, None, :, None]
    dr = np.arange(BIAS_SLOTS)[None, :, None, None] + (l // GRID_W)[None, None, None, :]
    shape = (NA_PAIRS, BIAS_SLOTS, 128, 128)
    return (np.broadcast_to(head, shape), np.broadcast_to(dr, shape),
            np.broadcast_to(dc[None, None], shape), np.broadcast_to(valid[None, None], shape))


def _rope_tables(T):
    half = RET_HEAD_DIM // 2
    inv = ROPE_BASE ** (-jnp.arange(half, dtype=jnp.float32) / half)
    ang = jnp.arange(T, dtype=jnp.float32)[:, None] * inv[None, :]
    cos, sin = jnp.cos(ang), jnp.sin(ang)
    return jnp.concatenate([cos, cos], axis=1), jnp.concatenate([-sin, sin], axis=1)


def _layer(x, mod3, shared):
    (gain, w_in, bd, qg, kg, cosf, sinf, bias, dmat, qf, qb, kf, kb, cdec, og, w_out) = shared
    T = x.shape[1]
    qa, ka, va, ga, qr, kr, vr, gr, sb = _in_proj(
        x, mod3, gain, w_in, bd, qg, kg, cosf[:T], sinf[:T], kb, cdec)
    return _mix_out(x, mod3, qa, ka, va, ga, qr, kr, vr, gr, sb, bias, dmat, qf, qb, kf, cdec,
                    og, w_out)


def kernel(x_prompt, x_sample, c_prompt, c_sample, norm_gain, w_ada, b_ada, w_in, na_q_gain,
           na_k_gain, na_rpb, ret_decay_f, ret_decay_b, ret_out_gain, w_out):
    depth = norm_gain.shape[0]
    bp, bs = x_prompt.shape[0], x_sample.shape[0]
    nb = -(-(bp + bs) // 8) * 8
    t_max = max(x_prompt.shape[1], x_sample.shape[1])
    cosf, sinf = _rope_tables(t_max)
    hid = np.arange(NA_WIDTH) // NA_HEAD_DIM
    bd = jnp.asarray(hid[:, None] == hid[None, :], jnp.bfloat16)
    head_i, dr_i, dc_i, valid = _bias_indices()
    c_all = jnp.concatenate(
        [c_prompt, c_sample, jnp.zeros((nb - bp - bs, D_MODEL), jnp.float32)], axis=0)
    y_prompt, y_sample = x_prompt, x_sample
    for l in range(depth):
        mod, dmat, qf, qb, kf, kb, cdec = _prep(c_all, w_ada[l], b_ada[l][None],
                                                ret_decay_f[l], ret_decay_b[l])
        mod3 = mod.reshape(nb, 3, D_MODEL)
        bias = jnp.where(valid, na_rpb[l][head_i, dr_i, dc_i], -jnp.inf)
        shared = (norm_gain[l][None], w_in[l].astype(jnp.bfloat16), bd,
                  jnp.tile(na_q_gain[l], NA_HEADS)[None], jnp.tile(na_k_gain[l], NA_HEADS)[None],
                  cosf, sinf, bias, dmat, qf, qb, kf, kb, cdec,
                  ret_out_gain[l].reshape(1, RET_WIDTH), w_out[l].astype(jnp.bfloat16))
        y_prompt = _layer(y_prompt, mod3[:bp], shared)
        y_sample = _layer(y_sample, mod3[bp:bp + bs], shared)
    return (y_prompt, y_sample)
```

```python
import functools

import numpy as np
import jax
import jax.numpy as jnp
from jax import lax
from jax.experimental import pallas as pl
from jax.experimental.pallas import tpu as pltpu

D_MODEL = 1024
GRID_W = 64
NA_HEADS = 8
NA_HEAD_DIM = 64
NA_WIDTH = NA_HEADS * NA_HEAD_DIM
NA_PAIRS = NA_HEADS // 2
NA_WIN_H = 8
NA_WIN_W = 16
RET_HEADS = 4
RET_HEAD_DIM = 128
RET_WIDTH = RET_HEADS * RET_HEAD_DIM
RET_CHUNK = 128
ROPE_BASE = 10000.0
NORM_EPS = 1e-6
IN_WIDTH = 4 * NA_WIDTH + 4 * RET_WIDTH
N_SEG = IN_WIDTH // 512

TOKEN_TILE = 512
HALO_ROWS = NA_WIN_H // 2
HALO_TOKENS = HALO_ROWS * GRID_W
BIAS_SLOTS = 2 * NA_WIN_H - 2
NA_GROUP = 16
VMEM_LIMIT_BYTES = 56 * 1024 * 1024

_NT = (((1,), (1,)), ((), ()))
_TN = (((0,), (0,)), ((), ()))


def _silu(v):
    return v / (1.0 + jnp.exp(-v))


def _prep_kernel(dec_f_ref, dec_b_ref, c_ref, w_ref, b_ref, rpb_ref,
                 mod_ref, dmat_ref, qf_ref, qb_ref, kf_ref, kb_ref, cdec_ref, bias_ref):
    c = c_ref[...]
    mod_ref[...] = jnp.dot(_silu(c), w_ref[...], preferred_element_type=jnp.float32) + b_ref[...]

    @pl.when(pl.program_id(0) == 0)
    def _():
        C = RET_CHUNK
        head = lax.broadcasted_iota(jnp.int32, (1, RET_WIDTH), 1) // RET_HEAD_DIM
        df = jnp.zeros((1, RET_WIDTH), jnp.float32)
        db = jnp.zeros((1, RET_WIDTH), jnp.float32)
        for h in range(RET_HEADS):
            df = jnp.where(head == h, dec_f_ref[h], df)
            db = jnp.where(head == h, dec_b_ref[h], db)
        lgf = -jnp.exp(df)
        lgb = -jnp.exp(db)
        pos = lax.broadcasted_iota(jnp.int32, (C, RET_WIDTH), 0).astype(jnp.float32)
        qf_ref[...] = jnp.exp(lgf * (pos + 1.0))
        qb_ref[...] = jnp.exp(lgb * (C - pos))
        kf_ref[...] = jnp.exp(lgf * (C - 1.0 - pos))
        kb_ref[...] = jnp.exp(lgb * pos)
        row = lax.broadcasted_iota(jnp.int32, (8, RET_WIDTH), 0)
        cdec_ref[...] = jnp.where(row == 0, jnp.exp(lgf * C), jnp.exp(lgb * C))
        ri = lax.broadcasted_iota(jnp.int32, (C, C), 0)
        ci = lax.broadcasted_iota(jnp.int32, (C, C), 1)
        diff = (ri - ci).astype(jnp.float32)
        for h in range(RET_HEADS):
            lf = lgf[:, h * RET_HEAD_DIM:(h + 1) * RET_HEAD_DIM]
            lb = lgb[:, h * RET_HEAD_DIM:(h + 1) * RET_HEAD_DIM]
            dmat_ref[h] = jnp.where(diff >= 0, jnp.exp(lf * jnp.maximum(diff, 0.0)),
                                    jnp.exp(lb * jnp.maximum(-diff, 0.0)))

        qcol = lax.broadcasted_iota(jnp.int32, (GRID_W, 128), 0)
        lane = lax.broadcasted_iota(jnp.int32, (GRID_W, 128), 1)
        kcol = lane % GRID_W
        wstart = jnp.clip(qcol - NA_WIN_W // 2, 0, GRID_W - NA_WIN_W)
        valid = (kcol >= wstart) & (kcol < wstart + NA_WIN_W)
        centre = 128 - (NA_WIN_W - 1)
        for h in range(NA_HEADS):
            for d in range(BIAS_SLOTS):
                even = pltpu.roll(jnp.broadcast_to(rpb_ref[h, d:d + 1, :], (GRID_W, 128)),
                                  centre, 1, stride=1, stride_axis=0)
                odd = pltpu.roll(jnp.broadcast_to(rpb_ref[h, d + 1:d + 2, :], (GRID_W, 128)),
                                 (centre + GRID_W) % 128, 1, stride=1, stride_axis=0)
                blk = jnp.where(valid, jnp.where(lane < GRID_W, even, odd), -jnp.inf)
                bias_ref[h // 2, d, (h % 2) * GRID_W:(h % 2 + 1) * GRID_W, :] = blk


def _prep(c_all, w_ada, b_ada, dec_f, dec_b, rpb):
    nb = c_all.shape[0]
    C = RET_CHUNK
    f32 = jnp.float32
    smem = pl.BlockSpec(memory_space=pltpu.SMEM)
    const2 = lambda j: (0, 0)
    return pl.pallas_call(
        _prep_kernel,
        grid=(3,),
        in_specs=[smem, smem,
                  pl.BlockSpec((nb, D_MODEL), const2),
                  pl.BlockSpec((D_MODEL, D_MODEL), lambda j: (0, j)),
                  pl.BlockSpec((1, D_MODEL), lambda j: (0, j)),
                  pl.BlockSpec((NA_HEADS, 2 * NA_WIN_H - 1, 128), lambda j: (0, 0, 0))],
        out_specs=[pl.BlockSpec((nb, D_MODEL), lambda j: (0, j)),
                   pl.BlockSpec((RET_HEADS, C, C), lambda j: (0, 0, 0)),
                   pl.BlockSpec((C, RET_WIDTH), const2),
                   pl.BlockSpec((C, RET_WIDTH), const2),
                   pl.BlockSpec((C, RET_WIDTH), const2),
                   pl.BlockSpec((C, RET_WIDTH), const2),
                   pl.BlockSpec((8, RET_WIDTH), const2),
                   pl.BlockSpec((NA_PAIRS, BIAS_SLOTS, 128, 128), lambda j: (0, 0, 0, 0))],
        out_shape=[jax.ShapeDtypeStruct((nb, 3 * D_MODEL), f32),
                   jax.ShapeDtypeStruct((RET_HEADS, C, C), f32),
                   jax.ShapeDtypeStruct((C, RET_WIDTH), f32),
                   jax.ShapeDtypeStruct((C, RET_WIDTH), f32),
                   jax.ShapeDtypeStruct((C, RET_WIDTH), f32),
                   jax.ShapeDtypeStruct((C, RET_WIDTH), f32),
                   jax.ShapeDtypeStruct((8, RET_WIDTH), f32),
                   jax.ShapeDtypeStruct((NA_PAIRS, BIAS_SLOTS, 128, 128), f32)],
        compiler_params=pltpu.CompilerParams(dimension_semantics=("arbitrary",),
                                             vmem_limit_bytes=VMEM_LIMIT_BYTES),
        name="prep",
    )(dec_f, dec_b, c_all, w_ada, b_ada, rpb)


def _in_proj_kernel(x_ref, mod_ref, gain_ref, w_ref, bd_ref, qg_ref, kg_ref, cos_ref, sin_ref,
                    kb_ref, cdec_ref,
                    qa_ref, ka_ref, va_ref, ga_ref, qr_ref, kr_ref, vr_ref, gr_ref, sb_ref,
                    state_ref):
    bf16, f32 = jnp.bfloat16, jnp.float32
    tt = x_ref.shape[0]

    @pl.when(pl.program_id(1) == 0)
    def _():
        state_ref[...] = jnp.zeros_like(state_ref)

    x = x_ref[...]
    ms = jnp.mean(x * x, axis=-1, keepdims=True)
    a = gain_ref[...] * (1.0 + mod_ref[1:2, :])
    hb = (x * lax.rsqrt(ms + NORM_EPS) * a + mod_ref[0:1, :]).astype(bf16)

    def seg(s):
        return jnp.dot(hb, w_ref[:, s * 512:(s + 1) * 512], preferred_element_type=f32)

    def head_norm(p, g):
        ss = jnp.dot((p * p).astype(bf16), bd_ref[...], preferred_element_type=f32)
        return p * lax.rsqrt(ss * (1.0 / NA_HEAD_DIM) + NORM_EPS) * g

    qa_ref[...] = head_norm(seg(0), qg_ref[...] * (NA_HEAD_DIM ** -0.5)).astype(bf16)
    ka_ref[...] = head_norm(seg(1), kg_ref[...]).astype(bf16)
    va_ref[...] = seg(2).astype(bf16)
    ga_ref[...] = _silu(seg(3)).astype(bf16)

    cosf = cos_ref[...]
    sinf = sin_ref[...]

    def rotary(p):
        outs = []
        for h in range(RET_HEADS):
            ph = p[:, h * RET_HEAD_DIM:(h + 1) * RET_HEAD_DIM]
            outs.append(ph * cosf + pltpu.roll(ph, RET_HEAD_DIM // 2, 1) * sinf)
        return jnp.concatenate(outs, axis=1)

    qr_ref[...] = rotary(seg(4)).astype(bf16)
    kr = rotary(seg(5)) * (RET_HEAD_DIM ** -0.5)
    kr_ref[...] = kr.astype(bf16)
    vb = seg(6).astype(bf16)
    vr_ref[...] = vb
    gr_ref[...] = _silu(seg(7)).astype(bf16)

    kb = kb_ref[...]
    C = RET_CHUNK
    for c in reversed(range(tt // C)):
        kd = (kr[c * C:(c + 1) * C, :] * kb).astype(bf16)
        for h in range(RET_HEADS):
            hs = slice(h * RET_HEAD_DIM, (h + 1) * RET_HEAD_DIM)
            s_old = state_ref[h]
            sb_ref[c, h] = s_old.astype(bf16)
            kv = lax.dot_general(kd[:, hs], vb[c * C:(c + 1) * C, hs], _TN,
                                 preferred_element_type=f32)
            state_ref[h] = s_old * cdec_ref[1:2, hs] + kv


def _in_proj(x, mod3, gain, w_in, bd, qg, kg, cosf, sinf, kb, cdec):
    B, T, _ = x.shape
    tt = TOKEN_TILE
    nt = T // tt
    C = RET_CHUNK
    rev = lambda b, i: (b, nt - 1 - i, 0)
    c2 = lambda b, i: (0, 0)
    tok_spec = pl.BlockSpec((None, tt, 512), rev)
    out_tok = jax.ShapeDtypeStruct((B, T, 512), jnp.bfloat16)
    return pl.pallas_call(
        _in_proj_kernel,
        grid=(B, nt),
        in_specs=[pl.BlockSpec((None, tt, D_MODEL), rev),
                  pl.BlockSpec((None, 3, D_MODEL), lambda b, i: (b, 0, 0)),
                  pl.BlockSpec((1, D_MODEL), c2),
                  pl.BlockSpec((D_MODEL, IN_WIDTH), c2),
                  pl.BlockSpec((512, 512), c2),
                  pl.BlockSpec((1, 512), c2),
                  pl.BlockSpec((1, 512), c2),
                  pl.BlockSpec((tt, RET_HEAD_DIM), lambda b, i: (nt - 1 - i, 0)),
                  pl.BlockSpec((tt, RET_HEAD_DIM), lambda b, i: (nt - 1 - i, 0)),
                  pl.BlockSpec((C, RET_WIDTH), c2),
                  pl.BlockSpec((8, RET_WIDTH), c2)],
        out_specs=[tok_spec] * 8 + [
            pl.BlockSpec((None, tt // C, RET_HEADS, C, C), lambda b, i: (b, nt - 1 - i, 0, 0, 0))],
        out_shape=[out_tok] * 8 + [
            jax.ShapeDtypeStruct((B, T // C, RET_HEADS, C, C), jnp.bfloat16)],
        scratch_shapes=[pltpu.VMEM((RET_HEADS, C, C), jnp.float32)],
        compiler_params=pltpu.CompilerParams(dimension_semantics=("arbitrary", "arbitrary"),
                                             vmem_limit_bytes=VMEM_LIMIT_BYTES),
        name="in_proj",
    )(x, mod3, gain, w_in, bd, qg, kg, cosf, sinf, kb, cdec)


def _mix_out_kernel(x_ref, mod_ref, qa_ref, ga_ref, kap_ref, ka_ref, kan_ref,
                    vap_ref, va_ref, van_ref, bias_ref,
                    qr_ref, kr_ref, vr_ref, gr_ref, sb_ref,
                    dmat_ref, qf_ref, qb_ref, kf_ref, cdec_ref, og_ref, wo_ref,
                    y_ref,
                    state_ref, kbuf_ref, vbuf_ref, mix_ref, s0_ref, s1_ref, p_ref, *, rows):
    bf16, f32 = jnp.bfloat16, jnp.float32
    tt = x_ref.shape[0]
    n_rows = tt // GRID_W
    i = pl.program_id(1)

    @pl.when(i == 0)
    def _():
        state_ref[...] = jnp.zeros_like(state_ref)

    kbuf_ref[0:HALO_TOKENS, :] = kap_ref[...]
    kbuf_ref[HALO_TOKENS:HALO_TOKENS + tt, :] = ka_ref[...]
    kbuf_ref[HALO_TOKENS + tt:, :] = kan_ref[...]
    vbuf_ref[0:HALO_TOKENS, :] = vap_ref[...]
    vbuf_ref[HALO_TOKENS:HALO_TOKENS + tt, :] = va_ref[...]
    vbuf_ref[HALO_TOKENS + tt:, :] = van_ref[...]

    lane = lax.broadcasted_iota(jnp.int32, (GRID_W, 2 * NA_HEAD_DIM), 1)
    low = lane < NA_HEAD_DIM
    win = NA_WIN_H * GRID_W
    ones = jnp.ones((win, 128), bf16)

    def window(r):
        grow = i * n_rows + r
        start = jnp.clip(grow - NA_WIN_H // 2, 0, rows - NA_WIN_H)
        w0 = pl.multiple_of((start - (i * n_rows - HALO_ROWS)) * GRID_W, GRID_W)
        return w0, grow - start

    def na_scores(r, s_ref):
        w0, off = window(r)
        q0 = pl.multiple_of(r * GRID_W, GRID_W)
        for p in range(NA_PAIRS):
            ls = slice(p * 128, (p + 1) * 128)
            q = qa_ref[pl.ds(q0, GRID_W), ls]
            q2 = jnp.concatenate([jnp.where(low, q, jnp.zeros_like(q)),
                                  jnp.where(low, jnp.zeros_like(q), q)], axis=0)
            kw = kbuf_ref[pl.ds(w0, win), ls]
            s = lax.dot_general(q2, kw, _NT, preferred_element_type=f32)
            for j in range(win // 128):
                s_ref[p, :, j * 128:(j + 1) * 128] = (
                    s[:, j * 128:(j + 1) * 128] + bias_ref[p, 2 * j - off + (NA_WIN_H - 1)])

    def na_output(r, s_ref):
        w0, _ = window(r)
        q0 = pl.multiple_of(r * GRID_W, GRID_W)
        for p in range(NA_PAIRS):
            ls = slice(p * 128, (p + 1) * 128)
            for g in range(128 // NA_GROUP):
                gs = slice(g * NA_GROUP, (g + 1) * NA_GROUP)
                s = s_ref[p, gs, :]
                m = jnp.max(s, axis=-1, keepdims=True)
                p_ref[p, gs, :] = jnp.exp(s - m).astype(bf16)
            vw = jnp.concatenate([vbuf_ref[pl.ds(w0, win), ls], ones], axis=1)
            o2 = jnp.dot(p_ref[p], vw, preferred_element_type=f32)
            o2 = o2[:, 0:128] / o2[:, 128:256]
            o = jnp.where(low, o2[0:GRID_W], o2[GRID_W:])
            gate = ga_ref[pl.ds(q0, GRID_W), ls].astype(f32)
            mix_ref[pl.ds(q0, GRID_W), ls] = (o * gate).astype(bf16)

    na_scores(0, s0_ref)

    def na_two_rows(u, carry):
        na_scores(2 * u + 1, s1_ref)
        na_output(2 * u, s0_ref)
        na_scores(2 * u + 2, s0_ref)
        na_output(2 * u + 1, s1_ref)
        return carry

    lax.fori_loop(0, n_rows // 2 - 1, na_two_rows, 0)
    na_scores(n_rows - 1, s1_ref)
    na_output(n_rows - 2, s0_ref)
    na_output(n_rows - 1, s1_ref)

    C = RET_CHUNK

    def ret_chunk(c, carry):
        t0 = pl.multiple_of(c * C, C)
        for h in range(RET_HEADS):
            hs = slice(h * RET_HEAD_DIM, (h + 1) * RET_HEAD_DIM)
            q = qr_ref[pl.ds(t0, C), hs]
            k = kr_ref[pl.ds(t0, C), hs]
            v = vr_ref[pl.ds(t0, C), hs]
            s = lax.dot_general(q, k, _NT, preferred_element_type=f32) * dmat_ref[h]
            o = jnp.dot(s.astype(bf16), v, preferred_element_type=f32)
            sf = state_ref[h]
            o = o + jnp.dot(q, sf.astype(bf16), preferred_element_type=f32) * qf_ref[:, hs]
            o = o + jnp.dot(q, sb_ref[c, h], preferred_element_type=f32) * qb_ref[:, hs]
            kd = (k.astype(f32) * kf_ref[:, hs]).astype(bf16)
            state_ref[h] = sf * cdec_ref[0:1, hs] + lax.dot_general(
                kd, v, _TN, preferred_element_type=f32)
            ms = jnp.mean(o * o, axis=-1, keepdims=True)
            rn = o * lax.rsqrt(ms + NORM_EPS) * og_ref[:, hs]
            g = gr_ref[pl.ds(t0, C), hs].astype(f32)
            mix_ref[pl.ds(t0, C), NA_WIDTH + h * RET_HEAD_DIM:NA_WIDTH + (h + 1) * RET_HEAD_DIM] = (
                rn * g).astype(bf16)
        return carry

    lax.fori_loop(0, tt // C, ret_chunk, 0, unroll=True)

    out = jnp.dot(mix_ref[...], wo_ref[...], preferred_element_type=f32)
    y_ref[...] = x_ref[...] + mod_ref[2:3, :] * out


def _mix_out(x, mod3, qa, ka, va, ga, qr, kr, vr, gr, sb, bias, dmat, qf, qb, kf, cdec, og, w_out):
    B, T, _ = x.shape
    tt = TOKEN_TILE
    nt = T // tt
    C = RET_CHUNK
    rows = T // GRID_W
    hb = tt // HALO_TOKENS
    n_halo = T // HALO_TOKENS
    tok = lambda b, i: (b, i, 0)
    prev = lambda b, i: (b, jnp.maximum(i * hb - 1, 0), 0)
    nxt = lambda b, i: (b, jnp.minimum((i + 1) * hb, n_halo - 1), 0)
    c2 = lambda b, i: (0, 0)
    tok_spec = pl.BlockSpec((None, tt, 512), tok)
    prev_spec = pl.BlockSpec((None, HALO_TOKENS, 512), prev)
    next_spec = pl.BlockSpec((None, HALO_TOKENS, 512), nxt)
    return pl.pallas_call(
        functools.partial(_mix_out_kernel, rows=rows),
        grid=(B, nt),
        in_specs=[pl.BlockSpec((None, tt, D_MODEL), tok),
                  pl.BlockSpec((None, 3, D_MODEL), lambda b, i: (b, 0, 0)),
                  tok_spec, tok_spec,
                  prev_spec, tok_spec, next_spec,
                  prev_spec, tok_spec, next_spec,
                  pl.BlockSpec((NA_PAIRS, BIAS_SLOTS, 128, 128), lambda b, i: (0, 0, 0, 0)),
                  tok_spec, tok_spec, tok_spec, tok_spec,
                  pl.BlockSpec((None, tt // C, RET_HEADS, C, C), lambda b, i: (b, i, 0, 0, 0)),
                  pl.BlockSpec((RET_HEADS, C, C), lambda b, i: (0, 0, 0)),
                  pl.BlockSpec((C, RET_WIDTH), c2),
                  pl.BlockSpec((C, RET_WIDTH), c2),
                  pl.BlockSpec((C, RET_WIDTH), c2),
                  pl.BlockSpec((8, RET_WIDTH), c2),
                  pl.BlockSpec((1, RET_WIDTH), c2),
                  pl.BlockSpec((D_MODEL, D_MODEL), c2)],
        out_specs=pl.BlockSpec((None, tt, D_MODEL), tok),
        out_shape=jax.ShapeDtypeStruct((B, T, D_MODEL), jnp.float32),
        scratch_shapes=[pltpu.VMEM((RET_HEADS, C, C), jnp.float32),
                        pltpu.VMEM((tt + 2 * HALO_TOKENS, 512), jnp.bfloat16),
                        pltpu.VMEM((tt + 2 * HALO_TOKENS, 512), jnp.bfloat16),
                        pltpu.VMEM((tt, D_MODEL), jnp.bfloat16),
                        pltpu.VMEM((NA_PAIRS, 128, NA_WIN_H * GRID_W), jnp.float32),
                        pltpu.VMEM((NA_PAIRS, 128, NA_WIN_H * GRID_W), jnp.float32),
                        pltpu.VMEM((NA_PAIRS, 128, NA_WIN_H * GRID_W), jnp.bfloat16)],
        compiler_params=pltpu.CompilerParams(dimension_semantics=("arbitrary", "arbitrary"),
                                             vmem_limit_bytes=VMEM_LIMIT_BYTES),
        name="mix_out",
    )(x, mod3, qa, ga, ka, ka, ka, va, va, va, bias, qr, kr, vr, gr, sb,
      dmat, qf, qb, kf, cdec, og, w_out)


def _rope_tables(T):
    half = RET_HEAD_DIM // 2
    inv = ROPE_BASE ** (-jnp.arange(half, dtype=jnp.float32) / half)
    ang = jnp.arange(T, dtype=jnp.float32)[:, None] * inv[None, :]
    cos, sin = jnp.cos(ang), jnp.sin(ang)
    return jnp.concatenate([cos, cos], axis=1), jnp.concatenate([-sin, sin], axis=1)


def _layer(x, mod3, shared):
    (gain, w_in, bd, qg, kg, cosf, sinf, bias, dmat, qf, qb, kf, kb, cdec, og, w_out) = shared
    T = x.shape[1]
    qa, ka, va, ga, qr, kr, vr, gr, sb = _in_proj(
        x, mod3, gain, w_in, bd, qg, kg, cosf[:T], sinf[:T], kb, cdec)
    return _mix_out(x, mod3, qa, ka, va, ga, qr, kr, vr, gr, sb, bias, dmat, qf, qb, kf, cdec,
                    og, w_out)


def kernel(x_prompt, x_sample, c_prompt, c_sample, norm_gain, w_ada, b_ada, w_in, na_q_gain,
           na_k_gain, na_rpb, ret_decay_f, ret_decay_b, ret_out_gain, w_out):
    depth = norm_gain.shape[0]
    bp, bs = x_prompt.shape[0], x_sample.shape[0]
    nb = -(-(bp + bs) // 8) * 8
    t_max = max(x_prompt.shape[1], x_sample.shape[1])
    cosf, sinf = _rope_tables(t_max)
    hid = np.arange(NA_WIDTH) // NA_HEAD_DIM
    bd = jnp.asarray(hid[:, None] == hid[None, :], jnp.bfloat16)
    c_all = jnp.concatenate(
        [c_prompt, c_sample, jnp.zeros((nb - bp - bs, D_MODEL), jnp.float32)], axis=0)
    y_prompt, y_sample = x_prompt, x_sample
    for l in range(depth):
        rpb = jnp.pad(na_rpb[l], ((0, 0), (0, 0), (0, 128 - (2 * NA_WIN_W - 1))))
        mod, dmat, qf, qb, kf, kb, cdec, bias = _prep(c_all, w_ada[l], b_ada[l][None],
                                                      ret_decay_f[l], ret_decay_b[l], rpb)
        mod3 = mod.reshape(nb, 3, D_MODEL)
        shared = (norm_gain[l][None], w_in[l].astype(jnp.bfloat16), bd,
                  jnp.tile(na_q_gain[l], NA_HEADS)[None], jnp.tile(na_k_gain[l], NA_HEADS)[None],
                  cosf, sinf, bias, dmat, qf, qb, kf, kb, cdec,
                  ret_out_gain[l].reshape(1, RET_WIDTH), w_out[l].astype(jnp.bfloat16))
        y_prompt = _layer(y_prompt, mod3[:bp], shared)
        y_sample = _layer(y_sample, mod3[bp:bp + bs], shared)
    return (y_prompt, y_sample)
```

```python
import functools

import numpy as np
import jax
import jax.numpy as jnp
from jax import lax
from jax.experimental import pallas as pl
from jax.experimental.pallas import tpu as pltpu

D_MODEL = 1024
GRID_W = 64
NA_HEADS = 8
NA_HEAD_DIM = 64
NA_WIDTH = NA_HEADS * NA_HEAD_DIM
NA_PAIRS = NA_HEADS // 2
NA_WIN_H = 8
NA_WIN_W = 16
RET_HEADS = 4
RET_HEAD_DIM = 128
RET_WIDTH = RET_HEADS * RET_HEAD_DIM
RET_CHUNK = 128
ROPE_BASE = 10000.0
NORM_EPS = 1e-6
IN_WIDTH = 4 * NA_WIDTH + 4 * RET_WIDTH

TOKEN_TILE = 512
HALO_ROWS = NA_WIN_H // 2
HALO_TOKENS = HALO_ROWS * GRID_W
BIAS_SLOTS = 2 * NA_WIN_H - 2
NA_GROUP = 16
VMEM_LIMIT_BYTES = 56 * 1024 * 1024

_NT = (((1,), (1,)), ((), ()))
_TN = (((0,), (0,)), ((), ()))


def _silu(v):
    return v / (1.0 + jnp.exp(-v))


def _prep_kernel(dec_f_ref, dec_b_ref, c_ref, w_ref, b_ref, rpb_ref,
                 mod_ref, dmat_ref, qf_ref, qb_ref, kf_ref, kb_ref, cdec_ref, bias_ref):
    c = c_ref[...]
    mod_ref[...] = jnp.dot(_silu(c), w_ref[...], preferred_element_type=jnp.float32) + b_ref[...]

    @pl.when(pl.program_id(0) == 0)
    def _():
        C = RET_CHUNK
        head = lax.broadcasted_iota(jnp.int32, (1, RET_WIDTH), 1) // RET_HEAD_DIM
        df = jnp.zeros((1, RET_WIDTH), jnp.float32)
        db = jnp.zeros((1, RET_WIDTH), jnp.float32)
        for h in range(RET_HEADS):
            df = jnp.where(head == h, dec_f_ref[h], df)
            db = jnp.where(head == h, dec_b_ref[h], db)
        lgf = -jnp.exp(df)
        lgb = -jnp.exp(db)
        pos = lax.broadcasted_iota(jnp.int32, (C, RET_WIDTH), 0).astype(jnp.float32)
        qf_ref[...] = jnp.exp(lgf * (pos + 1.0))
        qb_ref[...] = jnp.exp(lgb * (C - pos))
        kf_ref[...] = jnp.exp(lgf * (C - 1.0 - pos))
        kb_ref[...] = jnp.exp(lgb * pos)
        row = lax.broadcasted_iota(jnp.int32, (8, RET_WIDTH), 0)
        cdec_ref[...] = jnp.where(row == 0, jnp.exp(lgf * C), jnp.exp(lgb * C))
        ri = lax.broadcasted_iota(jnp.int32, (C, C), 0)
        ci = lax.broadcasted_iota(jnp.int32, (C, C), 1)
        diff = (ri - ci).astype(jnp.float32)
        for h in range(RET_HEADS):
            lf = lgf[:, h * RET_HEAD_DIM:(h + 1) * RET_HEAD_DIM]
            lb = lgb[:, h * RET_HEAD_DIM:(h + 1) * RET_HEAD_DIM]
            dmat_ref[h] = jnp.where(diff >= 0, jnp.exp(lf * jnp.maximum(diff, 0.0)),
                                    jnp.exp(lb * jnp.maximum(-diff, 0.0)))

        qcol = lax.broadcasted_iota(jnp.int32, (GRID_W, 128), 0)
        lane = lax.broadcasted_iota(jnp.int32, (GRID_W, 128), 1)
        kcol = lane % GRID_W
        wstart = jnp.clip(qcol - NA_WIN_W // 2, 0, GRID_W - NA_WIN_W)
        valid = (kcol >= wstart) & (kcol < wstart + NA_WIN_W)
        centre = 128 - (NA_WIN_W - 1)
        for h in range(NA_HEADS):
            for d in range(BIAS_SLOTS):
                even = pltpu.roll(jnp.broadcast_to(rpb_ref[h, d:d + 1, :], (GRID_W, 128)),
                                  centre, 1, stride=1, stride_axis=0)
                odd = pltpu.roll(jnp.broadcast_to(rpb_ref[h, d + 1:d + 2, :], (GRID_W, 128)),
                                 (centre + GRID_W) % 128, 1, stride=1, stride_axis=0)
                blk = jnp.where(valid, jnp.where(lane < GRID_W, even, odd), -jnp.inf)
                bias_ref[h // 2, d, (h % 2) * GRID_W:(h % 2 + 1) * GRID_W, :] = blk


def _prep(c_all, w_ada, b_ada, dec_f, dec_b, rpb):
    nb = c_all.shape[0]
    C = RET_CHUNK
    f32 = jnp.float32
    smem = pl.BlockSpec(memory_space=pltpu.SMEM)
    const2 = lambda j: (0, 0)
    return pl.pallas_call(
        _prep_kernel,
        grid=(3,),
        in_specs=[smem, smem,
                  pl.BlockSpec((nb, D_MODEL), const2),
                  pl.BlockSpec((D_MODEL, D_MODEL), lambda j: (0, j)),
                  pl.BlockSpec((1, D_MODEL), lambda j: (0, j)),
                  pl.BlockSpec((NA_HEADS, 2 * NA_WIN_H - 1, 128), lambda j: (0, 0, 0))],
        out_specs=[pl.BlockSpec((nb, D_MODEL), lambda j: (0, j)),
                   pl.BlockSpec((RET_HEADS, C, C), lambda j: (0, 0, 0)),
                   pl.BlockSpec((C, RET_WIDTH), const2),
                   pl.BlockSpec((C, RET_WIDTH), const2),
                   pl.BlockSpec((C, RET_WIDTH), const2),
                   pl.BlockSpec((C, RET_WIDTH), const2),
                   pl.BlockSpec((8, RET_WIDTH), const2),
                   pl.BlockSpec((NA_PAIRS, BIAS_SLOTS, 128, 128), lambda j: (0, 0, 0, 0))],
        out_shape=[jax.ShapeDtypeStruct((nb, 3 * D_MODEL), f32),
                   jax.ShapeDtypeStruct((RET_HEADS, C, C), f32),
                   jax.ShapeDtypeStruct((C, RET_WIDTH), f32),
                   jax.ShapeDtypeStruct((C, RET_WIDTH), f32),
                   jax.ShapeDtypeStruct((C, RET_WIDTH), f32),
                   jax.ShapeDtypeStruct((C, RET_WIDTH), f32),
                   jax.ShapeDtypeStruct((8, RET_WIDTH), f32),
                   jax.ShapeDtypeStruct((NA_PAIRS, BIAS_SLOTS, 128, 128), f32)],
        compiler_params=pltpu.CompilerParams(dimension_semantics=("arbitrary",),
                                             vmem_limit_bytes=VMEM_LIMIT_BYTES),
        name="prep",
    )(dec_f, dec_b, c_all, w_ada, b_ada, rpb)


def _in_proj_kernel(x_ref, mod_ref, gain_ref, w_ref, bd_ref, qg_ref, kg_ref, cos_ref, sin_ref,
                    base_ref, kb_ref, cdec_ref,
                    qa_ref, ka_ref, va_ref, ga_ref, qr_ref, kr_ref, vr_ref, gr_ref, sb_ref,
                    state_ref):
    bf16, f32 = jnp.bfloat16, jnp.float32
    tt = x_ref.shape[0]

    @pl.when(pl.program_id(1) == 0)
    def _():
        state_ref[...] = jnp.zeros_like(state_ref)

    x = x_ref[...]
    ms = jnp.mean(x * x, axis=-1, keepdims=True)
    a = gain_ref[...] * (1.0 + mod_ref[1:2, :])
    hb = (x * lax.rsqrt(ms + NORM_EPS) * a + mod_ref[0:1, :]).astype(bf16)

    def seg(s):
        return jnp.dot(hb, w_ref[:, s * 512:(s + 1) * 512], preferred_element_type=f32)

    def head_norm(p, g):
        pp = (p * p).astype(bf16)
        ss = jnp.concatenate(
            [jnp.dot(pp[:, c * 256:(c + 1) * 256], bd_ref[...], preferred_element_type=f32)
             for c in range(NA_WIDTH // 256)], axis=1)
        return p * lax.rsqrt(ss * (1.0 / NA_HEAD_DIM) + NORM_EPS) * g

    qa_ref[...] = head_norm(seg(0), qg_ref[...] * (NA_HEAD_DIM ** -0.5)).astype(bf16)
    ka_ref[...] = head_norm(seg(1), kg_ref[...]).astype(bf16)
    va_ref[...] = seg(2).astype(bf16)
    ga_ref[...] = _silu(seg(3)).astype(bf16)

    c0, s0 = base_ref[0:1, :], base_ref[1:2, :]
    cw, sw = cos_ref[...], sin_ref[...]
    hlane = lax.broadcasted_iota(jnp.int32, (1, RET_HEAD_DIM), 1)
    cosf = c0 * cw - s0 * sw
    sinf = (s0 * cw + c0 * sw) * jnp.where(hlane < RET_HEAD_DIM // 2, -1.0, 1.0)

    def rotary(p):
        outs = []
        for h in range(RET_HEADS):
            ph = p[:, h * RET_HEAD_DIM:(h + 1) * RET_HEAD_DIM]
            outs.append(ph * cosf + pltpu.roll(ph, RET_HEAD_DIM // 2, 1) * sinf)
        return jnp.concatenate(outs, axis=1)

    qr_ref[...] = rotary(seg(4)).astype(bf16)
    kr = rotary(seg(5)) * (RET_HEAD_DIM ** -0.5)
    kr_ref[...] = kr.astype(bf16)
    vb = seg(6).astype(bf16)
    vr_ref[...] = vb
    gr_ref[...] = _silu(seg(7)).astype(bf16)

    kb = kb_ref[...]
    C = RET_CHUNK
    for c in reversed(range(tt // C)):
        kd = (kr[c * C:(c + 1) * C, :] * kb).astype(bf16)
        for h in range(RET_HEADS):
            hs = slice(h * RET_HEAD_DIM, (h + 1) * RET_HEAD_DIM)
            s_old = state_ref[h]
            sb_ref[c, h] = s_old.astype(bf16)
            kv = lax.dot_general(kd[:, hs], vb[c * C:(c + 1) * C, hs], _TN,
                                 preferred_element_type=f32)
            state_ref[h] = s_old * cdec_ref[1:2, hs] + kv


def _in_proj(x, mod3, gain, w_in, bd, qg, kg, cosw, sinw, base, kb, cdec):
    B, T, _ = x.shape
    tt = TOKEN_TILE
    nt = T // tt
    C = RET_CHUNK
    rev = lambda b, i: (b, nt - 1 - i, 0)
    c2 = lambda b, i: (0, 0)
    tok_spec = pl.BlockSpec((None, tt, 512), rev)
    out_tok = jax.ShapeDtypeStruct((B, T, 512), jnp.bfloat16)
    return pl.pallas_call(
        _in_proj_kernel,
        grid=(B, nt),
        in_specs=[pl.BlockSpec((None, tt, D_MODEL), rev),
                  pl.BlockSpec((None, 3, D_MODEL), lambda b, i: (b, 0, 0)),
                  pl.BlockSpec((1, D_MODEL), c2),
                  pl.BlockSpec((D_MODEL, IN_WIDTH), c2),
                  pl.BlockSpec((256, 256), c2),
                  pl.BlockSpec((1, 512), c2),
                  pl.BlockSpec((1, 512), c2),
                  pl.BlockSpec((tt, RET_HEAD_DIM), c2),
                  pl.BlockSpec((tt, RET_HEAD_DIM), c2),
                  pl.BlockSpec((None, 2, RET_HEAD_DIM), lambda b, i: (nt - 1 - i, 0, 0)),
                  pl.BlockSpec((C, RET_WIDTH), c2),
                  pl.BlockSpec((8, RET_WIDTH), c2)],
        out_specs=[tok_spec] * 8 + [
            pl.BlockSpec((None, tt // C, RET_HEADS, C, C), lambda b, i: (b, nt - 1 - i, 0, 0, 0))],
        out_shape=[out_tok] * 8 + [
            jax.ShapeDtypeStruct((B, T // C, RET_HEADS, C, C), jnp.bfloat16)],
        scratch_shapes=[pltpu.VMEM((RET_HEADS, C, C), jnp.float32)],
        compiler_params=pltpu.CompilerParams(dimension_semantics=("arbitrary", "arbitrary"),
                                             vmem_limit_bytes=VMEM_LIMIT_BYTES),
        name="in_proj",
    )(x, mod3, gain, w_in, bd, qg, kg, cosw, sinw, base, kb, cdec)


def _mix_out_kernel(x_ref, mod_ref, qa_ref, ga_ref, kap_ref, ka_ref, kan_ref,
                    vap_ref, va_ref, van_ref, bias_ref,
                    qr_ref, kr_ref, vr_ref, gr_ref, sb_ref,
                    dmat_ref, qf_ref, qb_ref, kf_ref, cdec_ref, og_ref, wo_ref,
                    y_ref,
                    state_ref, kbuf_ref, vbuf_ref, mix_ref, s0_ref, s1_ref, p_ref, *, rows):
    bf16, f32 = jnp.bfloat16, jnp.float32
    tt = x_ref.shape[0]
    n_rows = tt // GRID_W
    i = pl.program_id(1)

    @pl.when(i == 0)
    def _():
        state_ref[...] = jnp.zeros_like(state_ref)

    kbuf_ref[0:HALO_TOKENS, :] = kap_ref[...]
    kbuf_ref[HALO_TOKENS:HALO_TOKENS + tt, :] = ka_ref[...]
    kbuf_ref[HALO_TOKENS + tt:, :] = kan_ref[...]
    vbuf_ref[0:HALO_TOKENS, :] = vap_ref[...]
    vbuf_ref[HALO_TOKENS:HALO_TOKENS + tt, :] = va_ref[...]
    vbuf_ref[HALO_TOKENS + tt:, :] = van_ref[...]

    lane = lax.broadcasted_iota(jnp.int32, (GRID_W, 2 * NA_HEAD_DIM), 1)
    low = lane < NA_HEAD_DIM
    win = NA_WIN_H * GRID_W
    ones = jnp.ones((win, 128), bf16)

    def window(r):
        grow = i * n_rows + r
        start = jnp.clip(grow - NA_WIN_H // 2, 0, rows - NA_WIN_H)
        w0 = pl.multiple_of((start - (i * n_rows - HALO_ROWS)) * GRID_W, GRID_W)
        return w0, grow - start

    def na_scores(r, s_ref):
        w0, off = window(r)
        q0 = pl.multiple_of(r * GRID_W, GRID_W)
        for p in range(NA_PAIRS):
            ls = slice(p * 128, (p + 1) * 128)
            q = qa_ref[pl.ds(q0, GRID_W), ls]
            q2 = jnp.concatenate([jnp.where(low, q, jnp.zeros_like(q)),
                                  jnp.where(low, jnp.zeros_like(q), q)], axis=0)
            kw = kbuf_ref[pl.ds(w0, win), ls]
            s = lax.dot_general(q2, kw, _NT, preferred_element_type=f32)
            for j in range(win // 128):
                s_ref[p, :, j * 128:(j + 1) * 128] = (
                    s[:, j * 128:(j + 1) * 128] + bias_ref[p, 2 * j - off + (NA_WIN_H - 1)])

    def na_output(r, s_ref):
        w0, _ = window(r)
        q0 = pl.multiple_of(r * GRID_W, GRID_W)
        for p in range(NA_PAIRS):
            ls = slice(p * 128, (p + 1) * 128)
            for g in range(128 // NA_GROUP):
                gs = slice(g * NA_GROUP, (g + 1) * NA_GROUP)
                s = s_ref[p, gs, :]
                m = jnp.max(s, axis=-1, keepdims=True)
                p_ref[p, gs, :] = jnp.exp(s - m).astype(bf16)
            vw = jnp.concatenate([vbuf_ref[pl.ds(w0, win), ls], ones], axis=1)
            o2 = jnp.dot(p_ref[p], vw, preferred_element_type=f32)
            o2 = o2[:, 0:128] / o2[:, 128:256]
            o = jnp.where(low, o2[0:GRID_W], o2[GRID_W:])
            gate = ga_ref[pl.ds(q0, GRID_W), ls].astype(f32)
            mix_ref[pl.ds(q0, GRID_W), ls] = (o * gate).astype(bf16)

    na_scores(0, s0_ref)

    def na_two_rows(u, carry):
        na_scores(2 * u + 1, s1_ref)
        na_output(2 * u, s0_ref)
        na_scores(2 * u + 2, s0_ref)
        na_output(2 * u + 1, s1_ref)
        return carry

    lax.fori_loop(0, n_rows // 2 - 1, na_two_rows, 0, unroll=True)
    na_scores(n_rows - 1, s1_ref)
    na_output(n_rows - 2, s0_ref)
    na_output(n_rows - 1, s1_ref)

    C = RET_CHUNK

    def ret_chunk(c, carry):
        t0 = pl.multiple_of(c * C, C)
        for h in range(RET_HEADS):
            hs = slice(h * RET_HEAD_DIM, (h + 1) * RET_HEAD_DIM)
            q = qr_ref[pl.ds(t0, C), hs]
            k = kr_ref[pl.ds(t0, C), hs]
            v = vr_ref[pl.ds(t0, C), hs]
            s = lax.dot_general(q, k, _NT, preferred_element_type=f32) * dmat_ref[h]
            o = jnp.dot(s.astype(bf16), v, preferred_element_type=f32)
            sf = state_ref[h]
            o = o + jnp.dot(q, sf.astype(bf16), preferred_element_type=f32) * qf_ref[:, hs]
            o = o + jnp.dot(q, sb_ref[c, h], preferred_element_type=f32) * qb_ref[:, hs]
            kd = (k.astype(f32) * kf_ref[:, hs]).astype(bf16)
            state_ref[h] = sf * cdec_ref[0:1, hs] + lax.dot_general(
                kd, v, _TN, preferred_element_type=f32)
            ms = jnp.mean(o * o, axis=-1, keepdims=True)
            rn = o * lax.rsqrt(ms + NORM_EPS) * og_ref[:, hs]
            g = gr_ref[pl.ds(t0, C), hs].astype(f32)
            mix_ref[pl.ds(t0, C), NA_WIDTH + h * RET_HEAD_DIM:NA_WIDTH + (h + 1) * RET_HEAD_DIM] = (
                rn * g).astype(bf16)
        return carry

    lax.fori_loop(0, tt // C, ret_chunk, 0, unroll=True)

    out = jnp.dot(mix_ref[...], wo_ref[...], preferred_element_type=f32)
    y_ref[...] = x_ref[...] + mod_ref[2:3, :] * out


def _mix_out(x, mod3, qa, ka, va, ga, qr, kr, vr, gr, sb, bias, dmat, qf, qb, kf, cdec, og, w_out):
    B, T, _ = x.shape
    tt = TOKEN_TILE
    nt = T // tt
    C = RET_CHUNK
    rows = T // GRID_W
    hb = tt // HALO_TOKENS
    n_halo = T // HALO_TOKENS
    tok = lambda b, i: (b, i, 0)
    prev = lambda b, i: (b, jnp.maximum(i * hb - 1, 0), 0)
    nxt = lambda b, i: (b, jnp.minimum((i + 1) * hb, n_halo - 1), 0)
    c2 = lambda b, i: (0, 0)
    tok_spec = pl.BlockSpec((None, tt, 512), tok)
    prev_spec = pl.BlockSpec((None, HALO_TOKENS, 512), prev)
    next_spec = pl.BlockSpec((None, HALO_TOKENS, 512), nxt)
    return pl.pallas_call(
        functools.partial(_mix_out_kernel, rows=rows),
        grid=(B, nt),
        in_specs=[pl.BlockSpec((None, tt, D_MODEL), tok),
                  pl.BlockSpec((None, 3, D_MODEL), lambda b, i: (b, 0, 0)),
                  tok_spec, tok_spec,
                  prev_spec, tok_spec, next_spec,
                  prev_spec, tok_spec, next_spec,
                  pl.BlockSpec((NA_PAIRS, BIAS_SLOTS, 128, 128), lambda b, i: (0, 0, 0, 0)),
                  tok_spec, tok_spec, tok_spec, tok_spec,
                  pl.BlockSpec((None, tt // C, RET_HEADS, C, C), lambda b, i: (b, i, 0, 0, 0)),
                  pl.BlockSpec((RET_HEADS, C, C), lambda b, i: (0, 0, 0)),
                  pl.BlockSpec((C, RET_WIDTH), c2),
                  pl.BlockSpec((C, RET_WIDTH), c2),
                  pl.BlockSpec((C, RET_WIDTH), c2),
                  pl.BlockSpec((8, RET_WIDTH), c2),
                  pl.BlockSpec((1, RET_WIDTH), c2),
                  pl.BlockSpec((D_MODEL, D_MODEL), c2)],
        out_specs=pl.BlockSpec((None, tt, D_MODEL), tok),
        out_shape=jax.ShapeDtypeStruct((B, T, D_MODEL), jnp.float32),
        scratch_shapes=[pltpu.VMEM((RET_HEADS, C, C), jnp.float32),
                        pltpu.VMEM((tt + 2 * HALO_TOKENS, 512), jnp.bfloat16),
                        pltpu.VMEM((tt + 2 * HALO_TOKENS, 512), jnp.bfloat16),
                        pltpu.VMEM((tt, D_MODEL), jnp.bfloat16),
                        pltpu.VMEM((NA_PAIRS, 128, NA_WIN_H * GRID_W), jnp.float32),
                        pltpu.VMEM((NA_PAIRS, 128, NA_WIN_H * GRID_W), jnp.float32),
                        pltpu.VMEM((NA_PAIRS, 128, NA_WIN_H * GRID_W), jnp.bfloat16)],
        compiler_params=pltpu.CompilerParams(dimension_semantics=("arbitrary", "arbitrary"),
                                             vmem_limit_bytes=VMEM_LIMIT_BYTES),
        name="mix_out",
    )(x, mod3, qa, ga, ka, ka, ka, va, va, va, bias, qr, kr, vr, gr, sb,
      dmat, qf, qb, kf, cdec, og, w_out)


def _rope_tables(positions):
    half = RET_HEAD_DIM // 2
    inv = ROPE_BASE ** (-jnp.arange(half, dtype=jnp.float32) / half)
    ang = positions.astype(jnp.float32)[:, None] * inv[None, :]
    cos, sin = jnp.cos(ang), jnp.sin(ang)
    return jnp.concatenate([cos, cos], axis=1), jnp.concatenate([sin, sin], axis=1)


def _layer(x, mod3, shared):
    (gain, w_in, bd, qg, kg, cosw, sinw, base, bias, dmat, qf, qb, kf, kb, cdec, og, w_out) = shared
    nt = x.shape[1] // TOKEN_TILE
    qa, ka, va, ga, qr, kr, vr, gr, sb = _in_proj(
        x, mod3, gain, w_in, bd, qg, kg, cosw, sinw, base[:nt], kb, cdec)
    return _mix_out(x, mod3, qa, ka, va, ga, qr, kr, vr, gr, sb, bias, dmat, qf, qb, kf, cdec,
                    og, w_out)


def kernel(x_prompt, x_sample, c_prompt, c_sample, norm_gain, w_ada, b_ada, w_in, na_q_gain,
           na_k_gain, na_rpb, ret_decay_f, ret_decay_b, ret_out_gain, w_out):
    depth = norm_gain.shape[0]
    bp, bs = x_prompt.shape[0], x_sample.shape[0]
    nb = -(-(bp + bs) // 8) * 8
    t_max = max(x_prompt.shape[1], x_sample.shape[1])
    cosw, sinw = _rope_tables(jnp.arange(TOKEN_TILE))
    base = jnp.stack(_rope_tables(jnp.arange(t_max // TOKEN_TILE) * TOKEN_TILE), axis=1)
    hid = np.arange(256) // NA_HEAD_DIM
    bd = jnp.asarray(hid[:, None] == hid[None, :], jnp.bfloat16)
    c_all = jnp.concatenate(
        [c_prompt, c_sample, jnp.zeros((nb - bp - bs, D_MODEL), jnp.float32)], axis=0)
    y_prompt, y_sample = x_prompt, x_sample
    for l in range(depth):
        rpb = jnp.pad(na_rpb[l], ((0, 0), (0, 0), (0, 128 - (2 * NA_WIN_W - 1))))
        mod, dmat, qf, qb, kf, kb, cdec, bias = _prep(c_all, w_ada[l], b_ada[l][None],
                                                      ret_decay_f[l], ret_decay_b[l], rpb)
        mod3 = mod.reshape(nb, 3, D_MODEL)
        shared = (norm_gain[l][None], w_in[l].astype(jnp.bfloat16), bd,
                  jnp.tile(na_q_gain[l], NA_HEADS)[None], jnp.tile(na_k_gain[l], NA_HEADS)[None],
                  cosw, sinw, base, bias, dmat, qf, qb, kf, kb, cdec,
                  ret_out_gain[l].reshape(1, RET_WIDTH), w_out[l].astype(jnp.bfloat16))
        y_prompt = _layer(y_prompt, mod3[:bp], shared)
        y_sample = _layer(y_sample, mod3[bp:bp + bs], shared)
    return (y_prompt, y_sample)
```

```python
import functools

import numpy as np
import jax
import jax.numpy as jnp
from jax import lax
from jax.experimental import pallas as pl
from jax.experimental.pallas import tpu as pltpu

D_MODEL = 1024
GRID_W = 64
NA_HEADS = 8
NA_HEAD_DIM = 64
NA_WIDTH = NA_HEADS * NA_HEAD_DIM
NA_PAIRS = NA_HEADS // 2
NA_WIN_H = 8
NA_WIN_W = 16
RET_HEADS = 4
RET_HEAD_DIM = 128
RET_WIDTH = RET_HEADS * RET_HEAD_DIM
RET_CHUNK = 128
ROPE_BASE = 10000.0
NORM_EPS = 1e-6
IN_WIDTH = 4 * NA_WIDTH + 4 * RET_WIDTH

TOKEN_TILE = 512
HALO_ROWS = NA_WIN_H // 2
HALO_TOKENS = HALO_ROWS * GRID_W
BIAS_SLOTS = 2 * NA_WIN_H - 2
NA_GROUP = 16
VMEM_LIMIT_BYTES = 56 * 1024 * 1024

_NT = (((1,), (1,)), ((), ()))
_TN = (((0,), (0,)), ((), ()))


def _silu(v):
    return v / (1.0 + jnp.exp(-v))


def _prep_kernel(dec_f_ref, dec_b_ref, c_ref, w_ref, b_ref, rpb_ref,
                 mod_ref, dmat_ref, qf_ref, qb_ref, kf_ref, kb_ref, cdec_ref, bias_ref):
    c = c_ref[...]
    mod_ref[...] = jnp.dot(_silu(c), w_ref[...], preferred_element_type=jnp.float32) + b_ref[...]

    @pl.when(pl.program_id(0) == 0)
    def _():
        C = RET_CHUNK
        head = lax.broadcasted_iota(jnp.int32, (1, RET_WIDTH), 1) // RET_HEAD_DIM
        df = jnp.zeros((1, RET_WIDTH), jnp.float32)
        db = jnp.zeros((1, RET_WIDTH), jnp.float32)
        for h in range(RET_HEADS):
            df = jnp.where(head == h, dec_f_ref[h], df)
            db = jnp.where(head == h, dec_b_ref[h], db)
        lgf = -jnp.exp(df)
        lgb = -jnp.exp(db)
        pos = lax.broadcasted_iota(jnp.int32, (C, RET_WIDTH), 0).astype(jnp.float32)
        qf_ref[...] = jnp.exp(lgf * (pos + 1.0))
        qb_ref[...] = jnp.exp(lgb * (C - pos))
        kf_ref[...] = jnp.exp(lgf * (C - 1.0 - pos))
        kb_ref[...] = jnp.exp(lgb * pos)
        row = lax.broadcasted_iota(jnp.int32, (8, RET_WIDTH), 0)
        cdec_ref[...] = jnp.where(row == 0, jnp.exp(lgf * C), jnp.exp(lgb * C))
        ri = lax.broadcasted_iota(jnp.int32, (C, C), 0)
        ci = lax.broadcasted_iota(jnp.int32, (C, C), 1)
        diff = (ri - ci).astype(jnp.float32)
        for h in range(RET_HEADS):
            lf = lgf[:, h * RET_HEAD_DIM:(h + 1) * RET_HEAD_DIM]
            lb = lgb[:, h * RET_HEAD_DIM:(h + 1) * RET_HEAD_DIM]
            dmat_ref[h] = jnp.where(diff >= 0, jnp.exp(lf * jnp.maximum(diff, 0.0)),
                                    jnp.exp(lb * jnp.maximum(-diff, 0.0)))

        qcol = lax.broadcasted_iota(jnp.int32, (GRID_W, 128), 0)
        lane = lax.broadcasted_iota(jnp.int32, (GRID_W, 128), 1)
        kcol = lane % GRID_W
        wstart = jnp.clip(qcol - NA_WIN_W // 2, 0, GRID_W - NA_WIN_W)
        valid = (kcol >= wstart) & (kcol < wstart + NA_WIN_W)
        centre = 128 - (NA_WIN_W - 1)
        for h in range(NA_HEADS):
            for d in range(BIAS_SLOTS):
                even = pltpu.roll(jnp.broadcast_to(rpb_ref[h, d:d + 1, :], (GRID_W, 128)),
                                  centre, 1, stride=1, stride_axis=0)
                odd = pltpu.roll(jnp.broadcast_to(rpb_ref[h, d + 1:d + 2, :], (GRID_W, 128)),
                                 (centre + GRID_W) % 128, 1, stride=1, stride_axis=0)
                blk = jnp.where(valid, jnp.where(lane < GRID_W, even, odd), -jnp.inf)
                bias_ref[h // 2, d, (h % 2) * GRID_W:(h % 2 + 1) * GRID_W, :] = blk


def _prep(c_all, w_ada, b_ada, dec_f, dec_b, rpb):
    nb = c_all.shape[0]
    C = RET_CHUNK
    f32 = jnp.float32
    smem = pl.BlockSpec(memory_space=pltpu.SMEM)
    const2 = lambda j: (0, 0)
    return pl.pallas_call(
        _prep_kernel,
        grid=(3,),
        in_specs=[smem, smem,
                  pl.BlockSpec((nb, D_MODEL), const2),
                  pl.BlockSpec((D_MODEL, D_MODEL), lambda j: (0, j)),
                  pl.BlockSpec((1, D_MODEL), lambda j: (0, j)),
                  pl.BlockSpec((NA_HEADS, 2 * NA_WIN_H - 1, 128), lambda j: (0, 0, 0))],
        out_specs=[pl.BlockSpec((nb, D_MODEL), lambda j: (0, j)),
                   pl.BlockSpec((RET_HEADS, C, C), lambda j: (0, 0, 0)),
                   pl.BlockSpec((C, RET_WIDTH), const2),
                   pl.BlockSpec((C, RET_WIDTH), const2),
                   pl.BlockSpec((C, RET_WIDTH), const2),
                   pl.BlockSpec((C, RET_WIDTH), const2),
                   pl.BlockSpec((8, RET_WIDTH), const2),
                   pl.BlockSpec((NA_PAIRS, BIAS_SLOTS, 128, 128), lambda j: (0, 0, 0, 0))],
        out_shape=[jax.ShapeDtypeStruct((nb, 3 * D_MODEL), f32),
                   jax.ShapeDtypeStruct((RET_HEADS, C, C), f32),
                   jax.ShapeDtypeStruct((C, RET_WIDTH), f32),
                   jax.ShapeDtypeStruct((C, RET_WIDTH), f32),
                   jax.ShapeDtypeStruct((C, RET_WIDTH), f32),
                   jax.ShapeDtypeStruct((C, RET_WIDTH), f32),
                   jax.ShapeDtypeStruct((8, RET_WIDTH), f32),
                   jax.ShapeDtypeStruct((NA_PAIRS, BIAS_SLOTS, 128, 128), f32)],
        compiler_params=pltpu.CompilerParams(dimension_semantics=("arbitrary",),
                                             vmem_limit_bytes=VMEM_LIMIT_BYTES),
        name="prep",
    )(dec_f, dec_b, c_all, w_ada, b_ada, rpb)


def _in_proj_kernel(x_ref, mod_ref, gain_ref, w_ref, bd_ref, qg_ref, kg_ref, cos_ref, sin_ref,
                    base_ref, kb_ref, cdec_ref,
                    qa_ref, ka_ref, va_ref, ga_ref, qr_ref, kr_ref, vr_ref, gr_ref, sb_ref,
                    state_ref):
    bf16, f32 = jnp.bfloat16, jnp.float32
    tt = x_ref.shape[0]

    @pl.when(pl.program_id(1) == 0)
    def _():
        state_ref[...] = jnp.zeros_like(state_ref)

    x = x_ref[...]
    ms = jnp.mean(x * x, axis=-1, keepdims=True)
    a = gain_ref[...] * (1.0 + mod_ref[1:2, :])
    hb = (x * lax.rsqrt(ms + NORM_EPS) * a + mod_ref[0:1, :]).astype(bf16)

    def seg(s):
        return jnp.dot(hb, w_ref[:, s * 512:(s + 1) * 512], preferred_element_type=f32)

    def head_norm(p, g):
        pp = (p * p).astype(bf16)
        ss = jnp.concatenate(
            [jnp.dot(pp[:, c * 256:(c + 1) * 256], bd_ref[...], preferred_element_type=f32)
             for c in range(NA_WIDTH // 256)], axis=1)
        return p * lax.rsqrt(ss * (1.0 / NA_HEAD_DIM) + NORM_EPS) * g

    qa_ref[...] = head_norm(seg(0), qg_ref[...] * (NA_HEAD_DIM ** -0.5)).astype(bf16)
    ka_ref[...] = head_norm(seg(1), kg_ref[...]).astype(bf16)
    va_ref[...] = seg(2).astype(bf16)
    ga_ref[...] = _silu(seg(3)).astype(bf16)

    c0, s0 = base_ref[0:1, :], base_ref[1:2, :]
    cw, sw = cos_ref[...], sin_ref[...]
    hlane = lax.broadcasted_iota(jnp.int32, (1, RET_HEAD_DIM), 1)
    cosf = c0 * cw - s0 * sw
    sinf = (s0 * cw + c0 * sw) * jnp.where(hlane < RET_HEAD_DIM // 2, -1.0, 1.0)

    def rotary(p):
        outs = []
        for h in range(RET_HEADS):
            ph = p[:, h * RET_HEAD_DIM:(h + 1) * RET_HEAD_DIM]
            outs.append(ph * cosf + pltpu.roll(ph, RET_HEAD_DIM // 2, 1) * sinf)
        return jnp.concatenate(outs, axis=1)

    qr_ref[...] = rotary(seg(4)).astype(bf16)
    kr = rotary(seg(5)) * (RET_HEAD_DIM ** -0.5)
    kr_ref[...] = kr.astype(bf16)
    vb = seg(6).astype(bf16)
    vr_ref[...] = vb
    gr_ref[...] = _silu(seg(7)).astype(bf16)

    kb = kb_ref[...]
    C = RET_CHUNK
    for c in reversed(range(tt // C)):
        kd = (kr[c * C:(c + 1) * C, :] * kb).astype(bf16)
        for h in range(RET_HEADS):
            hs = slice(h * RET_HEAD_DIM, (h + 1) * RET_HEAD_DIM)
            s_old = state_ref[h]
            sb_ref[c, h] = s_old.astype(bf16)
            kv = lax.dot_general(kd[:, hs], vb[c * C:(c + 1) * C, hs], _TN,
                                 preferred_element_type=f32)
            state_ref[h] = s_old * cdec_ref[1:2, hs] + kv


def _in_proj(x, mod3, gain, w_in, bd, qg, kg, cosw, sinw, base, kb, cdec):
    B, T, _ = x.shape
    tt = TOKEN_TILE
    nt = T // tt
    C = RET_CHUNK
    rev = lambda b, i: (b, nt - 1 - i, 0)
    c2 = lambda b, i: (0, 0)
    tok_spec = pl.BlockSpec((None, tt, 512), rev)
    out_tok = jax.ShapeDtypeStruct((B, T, 512), jnp.bfloat16)
    return pl.pallas_call(
        _in_proj_kernel,
        grid=(B, nt),
        in_specs=[pl.BlockSpec((None, tt, D_MODEL), rev),
                  pl.BlockSpec((None, 3, D_MODEL), lambda b, i: (b, 0, 0)),
                  pl.BlockSpec((1, D_MODEL), c2),
                  pl.BlockSpec((D_MODEL, IN_WIDTH), c2),
                  pl.BlockSpec((256, 256), c2),
                  pl.BlockSpec((1, 512), c2),
                  pl.BlockSpec((1, 512), c2),
                  pl.BlockSpec((tt, RET_HEAD_DIM), c2),
                  pl.BlockSpec((tt, RET_HEAD_DIM), c2),
                  pl.BlockSpec((None, 2, RET_HEAD_DIM), lambda b, i: (nt - 1 - i, 0, 0)),
                  pl.BlockSpec((C, RET_WIDTH), c2),
                  pl.BlockSpec((8, RET_WIDTH), c2)],
        out_specs=[tok_spec] * 8 + [
            pl.BlockSpec((None, tt // C, RET_HEADS, C, C), lambda b, i: (b, nt - 1 - i, 0, 0, 0))],
        out_shape=[out_tok] * 8 + [
            jax.ShapeDtypeStruct((B, T // C, RET_HEADS, C, C), jnp.bfloat16)],
        scratch_shapes=[pltpu.VMEM((RET_HEADS, C, C), jnp.float32)],
        compiler_params=pltpu.CompilerParams(dimension_semantics=("arbitrary", "arbitrary"),
                                             vmem_limit_bytes=VMEM_LIMIT_BYTES),
        name="in_proj",
    )(x, mod3, gain, w_in, bd, qg, kg, cosw, sinw, base, kb, cdec)


def _halo_start(i, tt, seq_len):
    units = jnp.clip(i * (tt // HALO_TOKENS) - 1, 0, (seq_len - tt) // HALO_TOKENS - 2)
    return units * HALO_TOKENS


def _mix_out_kernel(x_ref, mod_ref, qa_ref, ga_ref, kbuf_ref, vbuf_ref, bias_ref,
                    qr_ref, kr_ref, vr_ref, gr_ref, sb_ref,
                    dmat_ref, qf_ref, qb_ref, kf_ref, cdec_ref, og_ref, wo_ref,
                    y_ref,
                    state_ref, mix_ref, s0_ref, s1_ref, m0_ref, m1_ref,
                    p0_ref, p1_ref, *, rows):
    bf16, f32 = jnp.bfloat16, jnp.float32
    tt = x_ref.shape[0]
    n_rows = tt // GRID_W
    i = pl.program_id(1)

    @pl.when(i == 0)
    def _():
        state_ref[...] = jnp.zeros_like(state_ref)

    first_row = _halo_start(i, tt, rows * GRID_W) // GRID_W

    lane = lax.broadcasted_iota(jnp.int32, (GRID_W, 2 * NA_HEAD_DIM), 1)
    low = lane < NA_HEAD_DIM
    win = NA_WIN_H * GRID_W
    ones = jnp.ones((win, 128), bf16)

    def window(r):
        grow = i * n_rows + r
        start = jnp.clip(grow - NA_WIN_H // 2, 0, rows - NA_WIN_H)
        w0 = pl.multiple_of((start - first_row) * GRID_W, GRID_W)
        return w0, grow - start

    def na_scores(r, s_ref, m_ref):
        w0, off = window(r)
        q0 = pl.multiple_of(r * GRID_W, GRID_W)
        for p in range(NA_PAIRS):
            ls = slice(p * 128, (p + 1) * 128)
            q = qa_ref[pl.ds(q0, GRID_W), ls]
            q2 = jnp.concatenate([jnp.where(low, q, jnp.zeros_like(q)),
                                  jnp.where(low, jnp.zeros_like(q), q)], axis=0)
            kw = kbuf_ref[pl.ds(w0, win), ls]
            s = lax.dot_general(q2, kw, _NT, preferred_element_type=f32)
            blocks = [s[:, j * 128:(j + 1) * 128] + bias_ref[p, 2 * j - off + (NA_WIN_H - 1)]
                      for j in range(win // 128)]
            for j in range(win // 128):
                s_ref[p, :, j * 128:(j + 1) * 128] = blocks[j]
            m = jnp.maximum(jnp.maximum(blocks[0], blocks[1]), jnp.maximum(blocks[2], blocks[3]))
            m_ref[p] = jnp.broadcast_to(jnp.max(m, axis=-1, keepdims=True), (128, 128))

    def na_probs(s_ref, m_ref, p_ref):
        for p in range(NA_PAIRS):
            for g in range(128 // NA_GROUP):
                gs = slice(g * NA_GROUP, (g + 1) * NA_GROUP)
                m = m_ref[p, gs, :]
                for j in range(win // 128):
                    js = slice(j * 128, (j + 1) * 128)
                    p_ref[p, gs, js] = jnp.exp(s_ref[p, gs, js] - m).astype(bf16)

    def na_output(r, p_ref):
        w0, _ = window(r)
        q0 = pl.multiple_of(r * GRID_W, GRID_W)
        for p in range(NA_PAIRS):
            ls = slice(p * 128, (p + 1) * 128)
            vw = jnp.concatenate([vbuf_ref[pl.ds(w0, win), ls], ones], axis=1)
            o2 = jnp.dot(p_ref[p], vw, preferred_element_type=f32)
            o2 = o2[:, 0:128] / o2[:, 128:256]
            o = jnp.where(low, o2[0:GRID_W], o2[GRID_W:])
            gate = ga_ref[pl.ds(q0, GRID_W), ls].astype(f32)
            mix_ref[pl.ds(q0, GRID_W), ls] = (o * gate).astype(bf16)

    s_refs, m_refs, p_refs = (s0_ref, s1_ref), (m0_ref, m1_ref), (p0_ref, p1_ref)
    for t in range(n_rows + 2):
        if t < n_rows:
            na_scores(t, s_refs[t % 2], m_refs[t % 2])
        if 1 <= t <= n_rows:
            na_probs(s_refs[(t - 1) % 2], m_refs[(t - 1) % 2], p_refs[(t - 1) % 2])
        if t >= 2:
            na_output(t - 2, p_refs[t % 2])

    C = RET_CHUNK

    def ret_chunk(c, carry):
        t0 = pl.multiple_of(c * C, C)
        for h in range(RET_HEADS):
            hs = slice(h * RET_HEAD_DIM, (h + 1) * RET_HEAD_DIM)
            q = qr_ref[pl.ds(t0, C), hs]
            k = kr_ref[pl.ds(t0, C), hs]
            v = vr_ref[pl.ds(t0, C), hs]
            s = lax.dot_general(q, k, _NT, preferred_element_type=f32) * dmat_ref[h]
            o = jnp.dot(s.astype(bf16), v, preferred_element_type=f32)
            sf = state_ref[h]
            o = o + jnp.dot(q, sf.astype(bf16), preferred_element_type=f32) * qf_ref[:, hs]
            o = o + jnp.dot(q, sb_ref[c, h], preferred_element_type=f32) * qb_ref[:, hs]
            kd = (k.astype(f32) * kf_ref[:, hs]).astype(bf16)
            state_ref[h] = sf * cdec_ref[0:1, hs] + lax.dot_general(
                kd, v, _TN, preferred_element_type=f32)
            ms = jnp.mean(o * o, axis=-1, keepdims=True)
            rn = o * lax.rsqrt(ms + NORM_EPS) * og_ref[:, hs]
            g = gr_ref[pl.ds(t0, C), hs].astype(f32)
            mix_ref[pl.ds(t0, C), NA_WIDTH + h * RET_HEAD_DIM:NA_WIDTH + (h + 1) * RET_HEAD_DIM] = (
                rn * g).astype(bf16)
        return carry

    lax.fori_loop(0, tt // C, ret_chunk, 0, unroll=True)

    out = jnp.dot(mix_ref[...], wo_ref[...], preferred_element_type=f32)
    y_ref[...] = x_ref[...] + mod_ref[2:3, :] * out


def _mix_out(x, mod3, qa, ka, va, ga, qr, kr, vr, gr, sb, bias, dmat, qf, qb, kf, cdec, og, w_out):
    B, T, _ = x.shape
    tt = TOKEN_TILE
    nt = T // tt
    C = RET_CHUNK
    rows = T // GRID_W
    tok = lambda b, i: (b, i, 0)
    c2 = lambda b, i: (0, 0)
    tok_spec = pl.BlockSpec((None, tt, 512), tok)
    halo_spec = pl.BlockSpec((None, pl.Element(tt + 2 * HALO_TOKENS), pl.Element(512)),
                             lambda b, i: (b, _halo_start(i, tt, T), 0))
    return pl.pallas_call(
        functools.partial(_mix_out_kernel, rows=rows),
        grid=(B, nt),
        in_specs=[pl.BlockSpec((None, tt, D_MODEL), tok),
                  pl.BlockSpec((None, 3, D_MODEL), lambda b, i: (b, 0, 0)),
                  tok_spec, tok_spec,
                  halo_spec, halo_spec,
                  pl.BlockSpec((NA_PAIRS, BIAS_SLOTS, 128, 128), lambda b, i: (0, 0, 0, 0)),
                  tok_spec, tok_spec, tok_spec, tok_spec,
                  pl.BlockSpec((None, tt // C, RET_HEADS, C, C), lambda b, i: (b, i, 0, 0, 0)),
                  pl.BlockSpec((RET_HEADS, C, C), lambda b, i: (0, 0, 0)),
                  pl.BlockSpec((C, RET_WIDTH), c2),
                  pl.BlockSpec((C, RET_WIDTH), c2),
                  pl.BlockSpec((C, RET_WIDTH), c2),
                  pl.BlockSpec((8, RET_WIDTH), c2),
                  pl.BlockSpec((1, RET_WIDTH), c2),
                  pl.BlockSpec((D_MODEL, D_MODEL), c2)],
        out_specs=pl.BlockSpec((None, tt, D_MODEL), tok),
        out_shape=jax.ShapeDtypeStruct((B, T, D_MODEL), jnp.float32),
        scratch_shapes=[pltpu.VMEM((RET_HEADS, C, C), jnp.float32),
                        pltpu.VMEM((tt, D_MODEL), jnp.bfloat16),
                        pltpu.VMEM((NA_PAIRS, 128, NA_WIN_H * GRID_W), jnp.float32),
                        pltpu.VMEM((NA_PAIRS, 128, NA_WIN_H * GRID_W), jnp.float32),
                        pltpu.VMEM((NA_PAIRS, 128, 128), jnp.float32),
                        pltpu.VMEM((NA_PAIRS, 128, 128), jnp.float32),
                        pltpu.VMEM((NA_PAIRS, 128, NA_WIN_H * GRID_W), jnp.bfloat16),
                        pltpu.VMEM((NA_PAIRS, 128, NA_WIN_H * GRID_W), jnp.bfloat16)],
        compiler_params=pltpu.CompilerParams(dimension_semantics=("arbitrary", "arbitrary"),
                                             vmem_limit_bytes=VMEM_LIMIT_BYTES),
        name="mix_out",
    )(x, mod3, qa, ga, ka, va, bias, qr, kr, vr, gr, sb,
      dmat, qf, qb, kf, cdec, og, w_out)


def _rope_tables(positions):
    half = RET_HEAD_DIM // 2
    inv = ROPE_BASE ** (-jnp.arange(half, dtype=jnp.float32) / half)
    ang = positions.astype(jnp.float32)[:, None] * inv[None, :]
    cos, sin = jnp.cos(ang), jnp.sin(ang)
    return jnp.concatenate([cos, cos], axis=1), jnp.concatenate([sin, sin], axis=1)


def _layer(x, mod3, shared):
    (gain, w_in, bd, qg, kg, cosw, sinw, base, bias, dmat, qf, qb, kf, kb, cdec, og, w_out) = shared
    nt = x.shape[1] // TOKEN_TILE
    qa, ka, va, ga, qr, kr, vr, gr, sb = _in_proj(
        x, mod3, gain, w_in, bd, qg, kg, cosw, sinw, base[:nt], kb, cdec)
    return _mix_out(x, mod3, qa, ka, va, ga, qr, kr, vr, gr, sb, bias, dmat, qf, qb, kf, cdec,
                    og, w_out)


def kernel(x_prompt, x_sample, c_prompt, c_sample, norm_gain, w_ada, b_ada, w_in, na_q_gain,
           na_k_gain, na_rpb, ret_decay_f, ret_decay_b, ret_out_gain, w_out):
    depth = norm_gain.shape[0]
    bp, bs = x_prompt.shape[0], x_sample.shape[0]
    nb = -(-(bp + bs) // 8) * 8
    t_max = max(x_prompt.shape[1], x_sample.shape[1])
    cosw, sinw = _rope_tables(jnp.arange(TOKEN_TILE))
    base = jnp.stack(_rope_tables(jnp.arange(t_max // TOKEN_TILE) * TOKEN_TILE), axis=1)
    hid = np.arange(256) // NA_HEAD_DIM
    bd = jnp.asarray(hid[:, None] == hid[None, :], jnp.bfloat16)
    c_all = jnp.concatenate(
        [c_prompt, c_sample, jnp.zeros((nb - bp - bs, D_MODEL), jnp.float32)], axis=0)
    y_prompt, y_sample = x_prompt, x_sample
    for l in range(depth):
        rpb = jnp.pad(na_rpb[l], ((0, 0), (0, 0), (0, 128 - (2 * NA_WIN_W - 1))))
        mod, dmat, qf, qb, kf, kb, cdec, bias = _prep(c_all, w_ada[l], b_ada[l][None],
                                                      ret_decay_f[l], ret_decay_b[l], rpb)
        mod3 = mod.reshape(nb, 3, D_MODEL)
        shared = (norm_gain[l][None], w_in[l].astype(jnp.bfloat16), bd,
                  jnp.tile(na_q_gain[l], NA_HEADS)[None], jnp.tile(na_k_gain[l], NA_HEADS)[None],
                  cosw, sinw, base, bias, dmat, qf, qb, kf, kb, cdec,
                  ret_out_gain[l].reshape(1, RET_WIDTH), w_out[l].astype(jnp.bfloat16))
        y_prompt = _layer(y_prompt, mod3[:bp], shared)
        y_sample = _layer(y_sample, mod3[bp:bp + bs], shared)
    return (y_prompt, y_sample)
```

```python
import functools

import numpy as np
import jax
import jax.numpy as jnp
from jax import lax
from jax.experimental import pallas as pl
from jax.experimental.pallas import tpu as pltpu

D_MODEL = 1024
GRID_W = 64
NA_HEADS = 8
NA_HEAD_DIM = 64
NA_WIDTH = NA_HEADS * NA_HEAD_DIM
NA_PAIRS = NA_HEADS // 2
NA_WIN_H = 8
NA_WIN_W = 16
RET_HEADS = 4
RET_HEAD_DIM = 128
RET_WIDTH = RET_HEADS * RET_HEAD_DIM
RET_CHUNK = 128
ROPE_BASE = 10000.0
NORM_EPS = 1e-6
IN_WIDTH = 4 * NA_WIDTH + 4 * RET_WIDTH

TOKEN_TILE = 1024
HALO_ROWS = NA_WIN_H // 2
HALO_TOKENS = HALO_ROWS * GRID_W
BIAS_SLOTS = 2 * NA_WIN_H - 2
NA_GROUP = 16

TOK_QA, TOK_GA, TOK_QR, TOK_KR, TOK_VR, TOK_GR = (slice(s * 512, (s + 1) * 512) for s in range(6))
TOK_WIDTH = 6 * 512
DEC_QF, DEC_QB, DEC_KF, DEC_KB = range(4)
DEC_TABLES = 4
VMEM_LIMIT_BYTES = 56 * 1024 * 1024

_NT = (((1,), (1,)), ((), ()))
_TN = (((0,), (0,)), ((), ()))


def _silu(v):
    return v / (1.0 + jnp.exp(-v))


def _prep_kernel(dec_f_ref, dec_b_ref, c_ref, w_ref, b_ref, rpb_ref,
                 mod_ref, dmat_ref, dec_ref, cdec_ref, bias_ref):
    c = c_ref[...]
    mod_ref[...] = jnp.dot(_silu(c), w_ref[...], preferred_element_type=jnp.float32) + b_ref[...]

    @pl.when(pl.program_id(0) == 0)
    def _():
        C = RET_CHUNK
        head = lax.broadcasted_iota(jnp.int32, (1, RET_WIDTH), 1) // RET_HEAD_DIM
        df = jnp.zeros((1, RET_WIDTH), jnp.float32)
        db = jnp.zeros((1, RET_WIDTH), jnp.float32)
        for h in range(RET_HEADS):
            df = jnp.where(head == h, dec_f_ref[h], df)
            db = jnp.where(head == h, dec_b_ref[h], db)
        lgf = -jnp.exp(df)
        lgb = -jnp.exp(db)
        pos = lax.broadcasted_iota(jnp.int32, (C, RET_WIDTH), 0).astype(jnp.float32)
        dec_ref[DEC_QF] = jnp.exp(lgf * (pos + 1.0))
        dec_ref[DEC_QB] = jnp.exp(lgb * (C - pos))
        dec_ref[DEC_KF] = jnp.exp(lgf * (C - 1.0 - pos))
        dec_ref[DEC_KB] = jnp.exp(lgb * pos)
        row = lax.broadcasted_iota(jnp.int32, (8, RET_WIDTH), 0)
        cdec_ref[...] = jnp.where(row == 0, jnp.exp(lgf * C), jnp.exp(lgb * C))
        ri = lax.broadcasted_iota(jnp.int32, (C, C), 0)
        ci = lax.broadcasted_iota(jnp.int32, (C, C), 1)
        diff = (ri - ci).astype(jnp.float32)
        for h in range(RET_HEADS):
            lf = lgf[:, h * RET_HEAD_DIM:(h + 1) * RET_HEAD_DIM]
            lb = lgb[:, h * RET_HEAD_DIM:(h + 1) * RET_HEAD_DIM]
            dmat_ref[h] = jnp.where(diff >= 0, jnp.exp(lf * jnp.maximum(diff, 0.0)),
                                    jnp.exp(lb * jnp.maximum(-diff, 0.0)))

        qcol = lax.broadcasted_iota(jnp.int32, (GRID_W, 128), 0)
        lane = lax.broadcasted_iota(jnp.int32, (GRID_W, 128), 1)
        kcol = lane % GRID_W
        wstart = jnp.clip(qcol - NA_WIN_W // 2, 0, GRID_W - NA_WIN_W)
        valid = (kcol >= wstart) & (kcol < wstart + NA_WIN_W)
        centre = 128 - (NA_WIN_W - 1)
        for h in range(NA_HEADS):
            for d in range(BIAS_SLOTS):
                even = pltpu.roll(jnp.broadcast_to(rpb_ref[h, d:d + 1, :], (GRID_W, 128)),
                                  centre, 1, stride=1, stride_axis=0)
                odd = pltpu.roll(jnp.broadcast_to(rpb_ref[h, d + 1:d + 2, :], (GRID_W, 128)),
                                 (centre + GRID_W) % 128, 1, stride=1, stride_axis=0)
                blk = jnp.where(valid, jnp.where(lane < GRID_W, even, odd), -jnp.inf)
                bias_ref[h // 2, d, (h % 2) * GRID_W:(h % 2 + 1) * GRID_W, :] = blk


def _prep(c_all, w_ada, b_ada, dec_f, dec_b, rpb):
    nb = c_all.shape[0]
    C = RET_CHUNK
    f32 = jnp.float32
    smem = pl.BlockSpec(memory_space=pltpu.SMEM)
    const2 = lambda j: (0, 0)
    return pl.pallas_call(
        _prep_kernel,
        grid=(3,),
        in_specs=[smem, smem,
                  pl.BlockSpec((nb, D_MODEL), const2),
                  pl.BlockSpec((D_MODEL, D_MODEL), lambda j: (0, j)),
                  pl.BlockSpec((1, D_MODEL), lambda j: (0, j)),
                  pl.BlockSpec((NA_HEADS, 2 * NA_WIN_H - 1, 128), lambda j: (0, 0, 0))],
        out_specs=[pl.BlockSpec((nb, D_MODEL), lambda j: (0, j)),
                   pl.BlockSpec((RET_HEADS, C, C), lambda j: (0, 0, 0)),
                   pl.BlockSpec((DEC_TABLES, C, RET_WIDTH), lambda j: (0, 0, 0)),
                   pl.BlockSpec((8, RET_WIDTH), const2),
                   pl.BlockSpec((NA_PAIRS, BIAS_SLOTS, 128, 128), lambda j: (0, 0, 0, 0))],
        out_shape=[jax.ShapeDtypeStruct((nb, 3 * D_MODEL), f32),
                   jax.ShapeDtypeStruct((RET_HEADS, C, C), f32),
                   jax.ShapeDtypeStruct((DEC_TABLES, C, RET_WIDTH), f32),
                   jax.ShapeDtypeStruct((8, RET_WIDTH), f32),
                   jax.ShapeDtypeStruct((NA_PAIRS, BIAS_SLOTS, 128, 128), f32)],
        compiler_params=pltpu.CompilerParams(dimension_semantics=("arbitrary",),
                                             vmem_limit_bytes=VMEM_LIMIT_BYTES),
        name="prep",
    )(dec_f, dec_b, c_all, w_ada, b_ada, rpb)


def _in_proj_kernel(x_ref, mod_ref, gain_ref, w_ref, bd_ref, qg_ref, kg_ref, cos_ref, sin_ref,
                    base_ref, dec_ref, cdec_ref,
                    tok_ref, kv_ref, sb_ref,
                    state_ref):
    bf16, f32 = jnp.bfloat16, jnp.float32
    tt = x_ref.shape[0]

    @pl.when(pl.program_id(1) == 0)
    def _():
        state_ref[...] = jnp.zeros_like(state_ref)

    x = x_ref[...]
    ms = jnp.mean(x * x, axis=-1, keepdims=True)
    a = gain_ref[...] * (1.0 + mod_ref[1:2, :])
    hb = (x * lax.rsqrt(ms + NORM_EPS) * a + mod_ref[0:1, :]).astype(bf16)

    def seg(s):
        return jnp.dot(hb, w_ref[:, s * 512:(s + 1) * 512], preferred_element_type=f32)

    def head_norm(p, g):
        pp = (p * p).astype(bf16)
        ss = jnp.concatenate(
            [jnp.dot(pp[:, c * 256:(c + 1) * 256], bd_ref[...], preferred_element_type=f32)
             for c in range(NA_WIDTH // 256)], axis=1)
        return p * lax.rsqrt(ss * (1.0 / NA_HEAD_DIM) + NORM_EPS) * g

    tok_ref[:, TOK_QA] = head_norm(seg(0), qg_ref[...] * (NA_HEAD_DIM ** -0.5)).astype(bf16)
    kv_ref[:, 0:NA_WIDTH] = head_norm(seg(1), kg_ref[...]).astype(bf16)
    kv_ref[:, NA_WIDTH:] = seg(2).astype(bf16)
    tok_ref[:, TOK_GA] = _silu(seg(3)).astype(bf16)

    c0, s0 = base_ref[0:1, :], base_ref[1:2, :]
    cw, sw = cos_ref[...], sin_ref[...]
    hlane = lax.broadcasted_iota(jnp.int32, (1, RET_HEAD_DIM), 1)
    cosf = c0 * cw - s0 * sw
    sinf = (s0 * cw + c0 * sw) * jnp.where(hlane < RET_HEAD_DIM // 2, -1.0, 1.0)

    def rotary(p):
        outs = []
        for h in range(RET_HEADS):
            ph = p[:, h * RET_HEAD_DIM:(h + 1) * RET_HEAD_DIM]
            outs.append(ph * cosf + pltpu.roll(ph, RET_HEAD_DIM // 2, 1) * sinf)
        return jnp.concatenate(outs, axis=1)

    tok_ref[:, TOK_QR] = rotary(seg(4)).astype(bf16)
    kr = rotary(seg(5)) * (RET_HEAD_DIM ** -0.5)
    tok_ref[:, TOK_KR] = kr.astype(bf16)
    vb = seg(6).astype(bf16)
    tok_ref[:, TOK_VR] = vb
    tok_ref[:, TOK_GR] = _silu(seg(7)).astype(bf16)

    kb = dec_ref[DEC_KB]
    C = RET_CHUNK
    for c in reversed(range(tt // C)):
        kd = (kr[c * C:(c + 1) * C, :] * kb).astype(bf16)
        for h in range(RET_HEADS):
            hs = slice(h * RET_HEAD_DIM, (h + 1) * RET_HEAD_DIM)
            s_old = state_ref[h]
            sb_ref[c, h] = s_old.astype(bf16)
            kv = lax.dot_general(kd[:, hs], vb[c * C:(c + 1) * C, hs], _TN,
                                 preferred_element_type=f32)
            state_ref[h] = s_old * cdec_ref[1:2, hs] + kv


def _in_proj(x, mod3, gain, w_in, bd, qg, kg, cosw, sinw, base, dec, cdec):
    B, T, _ = x.shape
    tt = TOKEN_TILE
    nt = T // tt
    C = RET_CHUNK
    rev = lambda b, i: (b, nt - 1 - i, 0)
    c2 = lambda b, i: (0, 0)
    return pl.pallas_call(
        _in_proj_kernel,
        grid=(B, nt),
        in_specs=[pl.BlockSpec((None, tt, D_MODEL), rev),
                  pl.BlockSpec((None, 3, D_MODEL), lambda b, i: (b, 0, 0)),
                  pl.BlockSpec((1, D_MODEL), c2),
                  pl.BlockSpec((D_MODEL, IN_WIDTH), c2, pipeline_mode=pl.Buffered(1)),
                  pl.BlockSpec((256, 256), c2),
                  pl.BlockSpec((1, 512), c2),
                  pl.BlockSpec((1, 512), c2),
                  pl.BlockSpec((tt, RET_HEAD_DIM), c2),
                  pl.BlockSpec((tt, RET_HEAD_DIM), c2),
                  pl.BlockSpec((None, 2, RET_HEAD_DIM), lambda b, i: (nt - 1 - i, 0, 0)),
                  pl.BlockSpec((DEC_TABLES, C, RET_WIDTH), lambda b, i: (0, 0, 0)),
                  pl.BlockSpec((8, RET_WIDTH), c2)],
        out_specs=[pl.BlockSpec((None, tt, TOK_WIDTH), rev),
                   pl.BlockSpec((None, tt, 2 * NA_WIDTH), rev),
                   pl.BlockSpec((None, tt // C, RET_HEADS, C, C),
                                lambda b, i: (b, nt - 1 - i, 0, 0, 0))],
        out_shape=[jax.ShapeDtypeStruct((B, T, TOK_WIDTH), jnp.bfloat16),
                   jax.ShapeDtypeStruct((B, T, 2 * NA_WIDTH), jnp.bfloat16),
                   jax.ShapeDtypeStruct((B, T // C, RET_HEADS, C, C), jnp.bfloat16)],
        scratch_shapes=[pltpu.VMEM((RET_HEADS, C, C), jnp.float32)],
        compiler_params=pltpu.CompilerParams(dimension_semantics=("arbitrary", "arbitrary"),
                                             vmem_limit_bytes=VMEM_LIMIT_BYTES),
        name="in_proj",
    )(x, mod3, gain, w_in, bd, qg, kg, cosw, sinw, base, dec, cdec)


def _sub(outer, inner):
    return slice(outer.start + inner.start, outer.start + inner.stop)


def _halo_start(i, tt, seq_len):
    units = jnp.clip(i * (tt // HALO_TOKENS) - 1, 0, (seq_len - tt) // HALO_TOKENS - 2)
    return units * HALO_TOKENS


def _mix_out_kernel(x_ref, mod_ref, tok_ref, kv_ref, bias_ref, sb_ref,
                    dmat_ref, dec_ref, cdec_ref, og_ref, wo_ref,
                    y_ref,
                    state_ref, mix_ref, s0_ref, s1_ref, m0_ref, m1_ref,
                    p0_ref, p1_ref, *, rows):
    bf16, f32 = jnp.bfloat16, jnp.float32
    tt = x_ref.shape[0]
    n_rows = tt // GRID_W
    i = pl.program_id(1)

    @pl.when(i == 0)
    def _():
        state_ref[...] = jnp.zeros_like(state_ref)

    first_row = _halo_start(i, tt, rows * GRID_W) // GRID_W

    lane = lax.broadcasted_iota(jnp.int32, (GRID_W, 2 * NA_HEAD_DIM), 1)
    low = lane < NA_HEAD_DIM
    win = NA_WIN_H * GRID_W
    ones = jnp.ones((win, 128), bf16)

    def window(r):
        grow = i * n_rows + r
        start = jnp.clip(grow - NA_WIN_H // 2, 0, rows - NA_WIN_H)
        w0 = pl.multiple_of((start - first_row) * GRID_W, GRID_W)
        return w0, grow - start

    def na_scores(r, s_ref, m_ref):
        w0, off = window(r)
        q0 = pl.multiple_of(r * GRID_W, GRID_W)
        for p in range(NA_PAIRS):
            ls = slice(p * 128, (p + 1) * 128)
            q = tok_ref[pl.ds(q0, GRID_W), _sub(TOK_QA, ls)]
            q2 = jnp.concatenate([jnp.where(low, q, jnp.zeros_like(q)),
                                  jnp.where(low, jnp.zeros_like(q), q)], axis=0)
            kw = kv_ref[pl.ds(w0, win), ls]
            s = lax.dot_general(q2, kw, _NT, preferred_element_type=f32)
            blocks = [s[:, j * 128:(j + 1) * 128] + bias_ref[p, 2 * j - off + (NA_WIN_H - 1)]
                      for j in range(win // 128)]
            for j in range(win // 128):
                s_ref[p, :, j * 128:(j + 1) * 128] = blocks[j]
            m = jnp.maximum(jnp.maximum(blocks[0], blocks[1]), jnp.maximum(blocks[2], blocks[3]))
            m_ref[p] = jnp.broadcast_to(jnp.max(m, axis=-1, keepdims=True), (128, 128))

    def na_probs(s_ref, m_ref, p_ref):
        for p in range(NA_PAIRS):
            for g in range(128 // NA_GROUP):
                gs = slice(g * NA_GROUP, (g + 1) * NA_GROUP)
                m = m_ref[p, gs, :]
                for j in range(win // 128):
                    js = slice(j * 128, (j + 1) * 128)
                    p_ref[p, gs, js] = jnp.exp(s_ref[p, gs, js] - m).astype(bf16)

    def na_output(r, p_ref):
        w0, _ = window(r)
        q0 = pl.multiple_of(r * GRID_W, GRID_W)
        for p in range(NA_PAIRS):
            ls = slice(p * 128, (p + 1) * 128)
            vw = jnp.concatenate(
                [kv_ref[pl.ds(w0, win), NA_WIDTH + p * 128:NA_WIDTH + (p + 1) * 128], ones], axis=1)
            o2 = jnp.dot(p_ref[p], vw, preferred_element_type=f32)
            o2 = o2[:, 0:128] / o2[:, 128:256]
            o = jnp.where(low, o2[0:GRID_W], o2[GRID_W:])
            gate = tok_ref[pl.ds(q0, GRID_W), _sub(TOK_GA, ls)].astype(f32)
            mix_ref[pl.ds(q0, GRID_W), ls] = (o * gate).astype(bf16)

    s_refs, m_refs, p_refs = (s0_ref, s1_ref), (m0_ref, m1_ref), (p0_ref, p1_ref)
    for t in range(n_rows + 2):
        if t < n_rows:
            na_scores(t, s_refs[t % 2], m_refs[t % 2])
        if 1 <= t <= n_rows:
            na_probs(s_refs[(t - 1) % 2], m_refs[(t - 1) % 2], p_refs[(t - 1) % 2])
        if t >= 2:
            na_output(t - 2, p_refs[t % 2])

    C = RET_CHUNK

    def ret_chunk(c, carry):
        t0 = pl.multiple_of(c * C, C)
        for h in range(RET_HEADS):
            hs = slice(h * RET_HEAD_DIM, (h + 1) * RET_HEAD_DIM)
            q = tok_ref[pl.ds(t0, C), _sub(TOK_QR, hs)]
            k = tok_ref[pl.ds(t0, C), _sub(TOK_KR, hs)]
            v = tok_ref[pl.ds(t0, C), _sub(TOK_VR, hs)]
            s = lax.dot_general(q, k, _NT, preferred_element_type=f32) * dmat_ref[h]
            o = jnp.dot(s.astype(bf16), v, preferred_element_type=f32)
            sf = state_ref[h]
            o = o + jnp.dot(q, sf.astype(bf16), preferred_element_type=f32) * dec_ref[DEC_QF, :, hs]
            o = o + jnp.dot(q, sb_ref[c, h], preferred_element_type=f32) * dec_ref[DEC_QB, :, hs]
            kd = (k.astype(f32) * dec_ref[DEC_KF, :, hs]).astype(bf16)
            state_ref[h] = sf * cdec_ref[0:1, hs] + lax.dot_general(
                kd, v, _TN, preferred_element_type=f32)
            ms = jnp.mean(o * o, axis=-1, keepdims=True)
            rn = o * lax.rsqrt(ms + NORM_EPS) * og_ref[:, hs]
            g = tok_ref[pl.ds(t0, C), _sub(TOK_GR, hs)].astype(f32)
            mix_ref[pl.ds(t0, C), NA_WIDTH + h * RET_HEAD_DIM:NA_WIDTH + (h + 1) * RET_HEAD_DIM] = (
                rn * g).astype(bf16)
        return carry

    lax.fori_loop(0, tt // C, ret_chunk, 0, unroll=True)

    out = jnp.dot(mix_ref[...], wo_ref[...], preferred_element_type=f32)
    y_ref[...] = x_ref[...] + mod_ref[2:3, :] * out


def _mix_out(x, mod3, tok_arr, kv, sb, bias, dmat, dec, cdec, og, w_out):
    B, T, _ = x.shape
    tt = TOKEN_TILE
    nt = T // tt
    C = RET_CHUNK
    rows = T // GRID_W
    assert T % tt == 0 and T >= tt + 2 * HALO_TOKENS, (T, tt)
    tok = lambda b, i: (b, i, 0)
    c2 = lambda b, i: (0, 0)
    halo_spec =pl.BlockSpec((None, pl.Element(tt + 2 * HALO_TOKENS), pl.Element(2 * NA_WIDTH)),
                             lambda b, i: (b, _halo_start(i, tt, T), 0))
    return pl.pallas_call(
        functools.partial(_mix_out_kernel, rows=rows),
        grid=(B, nt),
        in_specs=[pl.BlockSpec((None, tt, D_MODEL), tok),
                  pl.BlockSpec((None, 3, D_MODEL), lambda b, i: (b, 0, 0)),
                  pl.BlockSpec((None, tt, TOK_WIDTH), tok),
                  halo_spec,
                  pl.BlockSpec((NA_PAIRS, BIAS_SLOTS, 128, 128), lambda b, i: (0, 0, 0, 0),
                               pipeline_mode=pl.Buffered(1)),
                  pl.BlockSpec((None, tt // C, RET_HEADS, C, C), lambda b, i: (b, i, 0, 0, 0)),
                  pl.BlockSpec((RET_HEADS, C, C), lambda b, i: (0, 0, 0)),
                  pl.BlockSpec((DEC_TABLES, C, RET_WIDTH), lambda b, i: (0, 0, 0)),
                  pl.BlockSpec((8, RET_WIDTH), c2),
                  pl.BlockSpec((1, RET_WIDTH), c2),
                  pl.BlockSpec((D_MODEL, D_MODEL), c2, pipeline_mode=pl.Buffered(1))],
        out_specs=pl.BlockSpec((None, tt, D_MODEL), tok),
        out_shape=jax.ShapeDtypeStruct((B, T, D_MODEL), jnp.float32),
        scratch_shapes=[pltpu.VMEM((RET_HEADS, C, C), jnp.float32),
                        pltpu.VMEM((tt, D_MODEL), jnp.bfloat16),
                        pltpu.VMEM((NA_PAIRS, 128, NA_WIN_H * GRID_W), jnp.float32),
                        pltpu.VMEM((NA_PAIRS, 128, NA_WIN_H * GRID_W), jnp.float32),
                        pltpu.VMEM((NA_PAIRS, 128, 128), jnp.float32),
                        pltpu.VMEM((NA_PAIRS, 128, 128), jnp.float32),
                        pltpu.VMEM((NA_PAIRS, 128, NA_WIN_H * GRID_W), jnp.bfloat16),
                        pltpu.VMEM((NA_PAIRS, 128, NA_WIN_H * GRID_W), jnp.bfloat16)],
        compiler_params=pltpu.CompilerParams(dimension_semantics=("arbitrary", "arbitrary"),
                                             vmem_limit_bytes=VMEM_LIMIT_BYTES),
        name="mix_out",
    )(x, mod3, tok_arr, kv, bias, sb, dmat, dec, cdec, og, w_out)


def _rope_tables(positions):
    half = RET_HEAD_DIM // 2
    inv = ROPE_BASE ** (-jnp.arange(half, dtype=jnp.float32) / half)
    ang = positions.astype(jnp.float32)[:, None] * inv[None, :]
    cos, sin = jnp.cos(ang), jnp.sin(ang)
    return jnp.concatenate([cos, cos], axis=1), jnp.concatenate([sin, sin], axis=1)


def _layer(x, mod3, shared):
    (gain, w_in, bd, qg, kg, cosw, sinw, base, bias, dmat, dec, cdec, og, w_out) = shared
    nt = x.shape[1] // TOKEN_TILE
    tok_arr, kv, sb = _in_proj(x, mod3, gain, w_in, bd, qg, kg, cosw, sinw, base[:nt], dec, cdec)
    return _mix_out(x, mod3, tok_arr, kv, sb, bias, dmat, dec, cdec, og, w_out)


def kernel(x_prompt, x_sample, c_prompt, c_sample, norm_gain, w_ada, b_ada, w_in, na_q_gain,
           na_k_gain, na_rpb, ret_decay_f, ret_decay_b, ret_out_gain, w_out):
    depth = norm_gain.shape[0]
    bp, bs = x_prompt.shape[0], x_sample.shape[0]
    nb = -(-(bp + bs) // 8) * 8
    t_max = max(x_prompt.shape[1], x_sample.shape[1])
    cosw, sinw = _rope_tables(jnp.arange(TOKEN_TILE))
    base = jnp.stack(_rope_tables(jnp.arange(t_max // TOKEN_TILE) * TOKEN_TILE), axis=1)
    hid = np.arange(256) // NA_HEAD_DIM
    bd = jnp.asarray(hid[:, None] == hid[None, :], jnp.bfloat16)
    c_all = jnp.concatenate(
        [c_prompt, c_sample, jnp.zeros((nb - bp - bs, D_MODEL), jnp.float32)], axis=0)
    y_prompt, y_sample = x_prompt, x_sample
    for l in range(depth):
        rpb = jnp.pad(na_rpb[l], ((0, 0), (0, 0), (0, 128 - (2 * NA_WIN_W - 1))))
        mod, dmat, dec, cdec, bias = _prep(c_all, w_ada[l], b_ada[l][None],
                                           ret_decay_f[l], ret_decay_b[l], rpb)
        mod3 = mod.reshape(nb, 3, D_MODEL)
        shared = (norm_gain[l][None], w_in[l].astype(jnp.bfloat16), bd,
                  jnp.tile(na_q_gain[l], NA_HEADS)[None], jnp.tile(na_k_gain[l], NA_HEADS)[None],
                  cosw, sinw, base, bias, dmat, dec, cdec,
                  ret_out_gain[l].reshape(1, RET_WIDTH), w_out[l].astype(jnp.bfloat16))
        y_prompt = _layer(y_prompt, mod3[:bp], shared)
        y_sample = _layer(y_sample, mod3[bp:bp + bs], shared)
    return (y_prompt, y_sample)
```

```python
import functools

import numpy as np
import jax
import jax.numpy as jnp
from jax import lax
from jax.experimental import pallas as pl
from jax.experimental.pallas import tpu as pltpu

D_MODEL = 1024
GRID_W = 64
NA_HEADS = 8
NA_HEAD_DIM = 64
NA_WIDTH = NA_HEADS * NA_HEAD_DIM
NA_PAIRS = NA_HEADS // 2
NA_WIN_H = 8
NA_WIN_W = 16
RET_HEADS = 4
RET_HEAD_DIM = 128
RET_WIDTH = RET_HEADS * RET_HEAD_DIM
RET_CHUNK = 128
ROPE_BASE = 10000.0
NORM_EPS = 1e-6
IN_WIDTH = 4 * NA_WIDTH + 4 * RET_WIDTH

PROJ_TILE = 1024
MIX_TILE = 512
HALO_ROWS = NA_WIN_H // 2
HALO_TOKENS = HALO_ROWS * GRID_W
BIAS_SLOTS = 2 * NA_WIN_H - 2
NA_GROUP = 16

TOK_QA, TOK_GA, TOK_QR, TOK_KR, TOK_VR, TOK_GR = (slice(s * 512, (s + 1) * 512) for s in range(6))
TOK_WIDTH = 6 * 512
DEC_QF, DEC_QB, DEC_KF, DEC_KB = range(4)
DEC_TABLES = 4
VMEM_LIMIT_BYTES = 56 * 1024 * 1024

_NT = (((1,), (1,)), ((), ()))
_TN = (((0,), (0,)), ((), ()))


def _silu(v):
    return v / (1.0 + jnp.exp(-v))


def _prep_kernel(dec_f_ref, dec_b_ref, c_ref, w_ref, b_ref, rpb_ref,
                 mod_ref, dmat_ref, dec_ref, cdec_ref, bias_ref):
    c = c_ref[...]
    mod_ref[...] = jnp.dot(_silu(c), w_ref[...], preferred_element_type=jnp.float32) + b_ref[...]

    @pl.when(pl.program_id(0) == 0)
    def _():
        C = RET_CHUNK
        head = lax.broadcasted_iota(jnp.int32, (1, RET_WIDTH), 1) // RET_HEAD_DIM
        df = jnp.zeros((1, RET_WIDTH), jnp.float32)
        db = jnp.zeros((1, RET_WIDTH), jnp.float32)
        for h in range(RET_HEADS):
            df = jnp.where(head == h, dec_f_ref[h], df)
            db = jnp.where(head == h, dec_b_ref[h], db)
        lgf = -jnp.exp(df)
        lgb = -jnp.exp(db)
        pos = lax.broadcasted_iota(jnp.int32, (C, RET_WIDTH), 0).astype(jnp.float32)
        dec_ref[DEC_QF] = jnp.exp(lgf * (pos + 1.0))
        dec_ref[DEC_QB] = jnp.exp(lgb * (C - pos))
        dec_ref[DEC_KF] = jnp.exp(lgf * (C - 1.0 - pos))
        dec_ref[DEC_KB] = jnp.exp(lgb * pos)
        row = lax.broadcasted_iota(jnp.int32, (8, RET_WIDTH), 0)
        cdec_ref[...] = jnp.where(row == 0, jnp.exp(lgf * C), jnp.exp(lgb * C))
        ri = lax.broadcasted_iota(jnp.int32, (C, C), 0)
        ci = lax.broadcasted_iota(jnp.int32, (C, C), 1)
        diff = (ri - ci).astype(jnp.float32)
        for h in range(RET_HEADS):
            lf = lgf[:, h * RET_HEAD_DIM:(h + 1) * RET_HEAD_DIM]
            lb = lgb[:, h * RET_HEAD_DIM:(h + 1) * RET_HEAD_DIM]
            dmat_ref[h] = jnp.where(diff >= 0, jnp.exp(lf * jnp.maximum(diff, 0.0)),
                                    jnp.exp(lb * jnp.maximum(-diff, 0.0)))

        qcol = lax.broadcasted_iota(jnp.int32, (GRID_W, 128), 0)
        lane = lax.broadcasted_iota(jnp.int32, (GRID_W, 128), 1)
        kcol = lane % GRID_W
        wstart = jnp.clip(qcol - NA_WIN_W // 2, 0, GRID_W - NA_WIN_W)
        valid = (kcol >= wstart) & (kcol < wstart + NA_WIN_W)
        centre = 128 - (NA_WIN_W - 1)
        for h in range(NA_HEADS):
            for d in range(BIAS_SLOTS):
                even = pltpu.roll(jnp.broadcast_to(rpb_ref[h, d:d + 1, :], (GRID_W, 128)),
                                  centre, 1, stride=1, stride_axis=0)
                odd = pltpu.roll(jnp.broadcast_to(rpb_ref[h, d + 1:d + 2, :], (GRID_W, 128)),
                                 (centre + GRID_W) % 128, 1, stride=1, stride_axis=0)
                blk = jnp.where(valid, jnp.where(lane < GRID_W, even, odd), -jnp.inf)
                bias_ref[h // 2, d, (h % 2) * GRID_W:(h % 2 + 1) * GRID_W, :] = blk


def _prep(c_all, w_ada, b_ada, dec_f, dec_b, rpb):
    nb = c_all.shape[0]
    C = RET_CHUNK
    f32 = jnp.float32
    smem = pl.BlockSpec(memory_space=pltpu.SMEM)
    const2 = lambda j: (0, 0)
    return pl.pallas_call(
        _prep_kernel,
        grid=(3,),
        in_specs=[smem, smem,
                  pl.BlockSpec((nb, D_MODEL), const2),
                  pl.BlockSpec((D_MODEL, D_MODEL), lambda j: (0, j)),
                  pl.BlockSpec((1, D_MODEL), lambda j: (0, j)),
                  pl.BlockSpec((NA_HEADS, 2 * NA_WIN_H - 1, 128), lambda j: (0, 0, 0))],
        out_specs=[pl.BlockSpec((nb, D_MODEL), lambda j: (0, j)),
                   pl.BlockSpec((RET_HEADS, C, C), lambda j: (0, 0, 0)),
                   pl.BlockSpec((DEC_TABLES, C, RET_WIDTH), lambda j: (0, 0, 0)),
                   pl.BlockSpec((8, RET_WIDTH), const2),
                   pl.BlockSpec((NA_PAIRS, BIAS_SLOTS, 128, 128), lambda j: (0, 0, 0, 0))],
        out_shape=[jax.ShapeDtypeStruct((nb, 3 * D_MODEL), f32),
                   jax.ShapeDtypeStruct((RET_HEADS, C, C), f32),
                   jax.ShapeDtypeStruct((DEC_TABLES, C, RET_WIDTH), f32),
                   jax.ShapeDtypeStruct((8, RET_WIDTH), f32),
                   jax.ShapeDtypeStruct((NA_PAIRS, BIAS_SLOTS, 128, 128), f32)],
        compiler_params=pltpu.CompilerParams(dimension_semantics=("arbitrary",),
                                             vmem_limit_bytes=VMEM_LIMIT_BYTES),
        name="prep",
    )(dec_f, dec_b, c_all, w_ada, b_ada, rpb)


def _in_proj_kernel(x_ref, mod_ref, gain_ref, w_ref, bd_ref, qg_ref, kg_ref, cos_ref, sin_ref,
                    base_ref, dec_ref, cdec_ref,
                    tok_ref, kv_ref, sb_ref,
                    state_ref):
    bf16, f32 = jnp.bfloat16, jnp.float32
    tt = x_ref.shape[0]

    @pl.when(pl.program_id(1) == 0)
    def _():
        state_ref[...] = jnp.zeros_like(state_ref)

    x = x_ref[...]
    ms = jnp.mean(x * x, axis=-1, keepdims=True)
    a = gain_ref[...] * (1.0 + mod_ref[1:2, :])
    hb = (x * lax.rsqrt(ms + NORM_EPS) * a + mod_ref[0:1, :]).astype(bf16)

    def seg(s):
        return jnp.dot(hb, w_ref[:, s * 512:(s + 1) * 512], preferred_element_type=f32)

    def head_norm(p, g):
        pp = (p * p).astype(bf16)
        ss = jnp.concatenate(
            [jnp.dot(pp[:, c * 256:(c + 1) * 256], bd_ref[...], preferred_element_type=f32)
             for c in range(NA_WIDTH // 256)], axis=1)
        return p * lax.rsqrt(ss * (1.0 / NA_HEAD_DIM) + NORM_EPS) * g

    tok_ref[:, TOK_QA] = head_norm(seg(0), qg_ref[...] * (NA_HEAD_DIM ** -0.5)).astype(bf16)
    kv_ref[:, 0:NA_WIDTH] = head_norm(seg(1), kg_ref[...]).astype(bf16)
    kv_ref[:, NA_WIDTH:] = seg(2).astype(bf16)
    tok_ref[:, TOK_GA] = _silu(seg(3)).astype(bf16)

    c0, s0 = base_ref[0:1, :], base_ref[1:2, :]
    cw, sw = cos_ref[...], sin_ref[...]
    hlane = lax.broadcasted_iota(jnp.int32, (1, RET_HEAD_DIM), 1)
    cosf = c0 * cw - s0 * sw
    sinf = (s0 * cw + c0 * sw) * jnp.where(hlane < RET_HEAD_DIM // 2, -1.0, 1.0)

    def rotary(p):
        outs = []
        for h in range(RET_HEADS):
            ph = p[:, h * RET_HEAD_DIM:(h + 1) * RET_HEAD_DIM]
            outs.append(ph * cosf + pltpu.roll(ph, RET_HEAD_DIM // 2, 1) * sinf)
        return jnp.concatenate(outs, axis=1)

    tok_ref[:, TOK_QR] = rotary(seg(4)).astype(bf16)
    kr = rotary(seg(5)) * (RET_HEAD_DIM ** -0.5)
    tok_ref[:, TOK_KR] = kr.astype(bf16)
    vb = seg(6).astype(bf16)
    tok_ref[:, TOK_VR] = vb
    tok_ref[:, TOK_GR] = _silu(seg(7)).astype(bf16)

    kb = dec_ref[DEC_KB]
    C = RET_CHUNK
    for c in reversed(range(tt // C)):
        kd = (kr[c * C:(c + 1) * C, :] * kb).astype(bf16)
        for h in range(RET_HEADS):
            hs = slice(h * RET_HEAD_DIM, (h + 1) * RET_HEAD_DIM)
            s_old = state_ref[h]
            sb_ref[c, h] = s_old.astype(bf16)
            kv = lax.dot_general(kd[:, hs], vb[c * C:(c + 1) * C, hs], _TN,
                                 preferred_element_type=f32)
            state_ref[h] = s_old * cdec_ref[1:2, hs] + kv


def _in_proj(x, mod3, gain, w_in, bd, qg, kg, cosw, sinw, base, dec, cdec):
    B, T, _ = x.shape
    tt = PROJ_TILE
    nt = T // tt
    C = RET_CHUNK
    rev = lambda b, i: (b, nt - 1 - i, 0)
    c2 = lambda b, i: (0, 0)
    return pl.pallas_call(
        _in_proj_kernel,
        grid=(B, nt),
        in_specs=[pl.BlockSpec((None, tt, D_MODEL), rev),
                  pl.BlockSpec((None, 3, D_MODEL), lambda b, i: (b, 0, 0)),
                  pl.BlockSpec((1, D_MODEL), c2),
                  pl.BlockSpec((D_MODEL, IN_WIDTH), c2, pipeline_mode=pl.Buffered(1)),
                  pl.BlockSpec((256, 256), c2),
                  pl.BlockSpec((1, 512), c2),
                  pl.BlockSpec((1, 512), c2),
                  pl.BlockSpec((tt, RET_HEAD_DIM), c2),
                  pl.BlockSpec((tt, RET_HEAD_DIM), c2),
                  pl.BlockSpec((None, 2, RET_HEAD_DIM), lambda b, i: (nt - 1 - i, 0, 0)),
                  pl.BlockSpec((DEC_TABLES, C, RET_WIDTH), lambda b, i: (0, 0, 0)),
                  pl.BlockSpec((8, RET_WIDTH), c2)],
        out_specs=[pl.BlockSpec((None, tt, TOK_WIDTH), rev),
                   pl.BlockSpec((None, tt, 2 * NA_WIDTH), rev),
                   pl.BlockSpec((None, tt // C, RET_HEADS, C, C),
                                lambda b, i: (b, nt - 1 - i, 0, 0, 0))],
        out_shape=[jax.ShapeDtypeStruct((B, T, TOK_WIDTH), jnp.bfloat16),
                   jax.ShapeDtypeStruct((B, T, 2 * NA_WIDTH), jnp.bfloat16),
                   jax.ShapeDtypeStruct((B, T // C, RET_HEADS, C, C), jnp.bfloat16)],
        scratch_shapes=[pltpu.VMEM((RET_HEADS, C, C), jnp.float32)],
        compiler_params=pltpu.CompilerParams(dimension_semantics=("arbitrary", "arbitrary"),
                                             vmem_limit_bytes=VMEM_LIMIT_BYTES),
        name="in_proj",
    )(x, mod3, gain, w_in, bd, qg, kg, cosw, sinw, base, dec, cdec)


def _sub(outer, inner):
    return slice(outer.start + inner.start, outer.start + inner.stop)


def _halo_start(i, tt, seq_len):
    units = jnp.clip(i * (tt // HALO_TOKENS) - 1, 0, (seq_len - tt) // HALO_TOKENS - 2)
    return units * HALO_TOKENS


def _mix_out_kernel(x_ref, mod_ref, tok_ref, kv_ref, bias_ref, sb_ref,
                    dmat_ref, dec_ref, cdec_ref, og_ref, wo_ref,
                    y_ref,
                    state_ref, mix_ref, s0_ref, s1_ref, m0_ref, m1_ref,
                    p0_ref, p1_ref, *, rows):
    bf16, f32 = jnp.bfloat16, jnp.float32
    tt = x_ref.shape[0]
    n_rows = tt // GRID_W
    i = pl.program_id(1)

    @pl.when(i == 0)
    def _():
        state_ref[...] = jnp.zeros_like(state_ref)

    first_row = _halo_start(i, tt, rows * GRID_W) // GRID_W

    lane = lax.broadcasted_iota(jnp.int32, (GRID_W, 2 * NA_HEAD_DIM), 1)
    low = lane < NA_HEAD_DIM
    win = NA_WIN_H * GRID_W
    ones = jnp.ones((win, 128), bf16)

    def window(r):
        grow = i * n_rows + r
        start = jnp.clip(grow - NA_WIN_H // 2, 0, rows - NA_WIN_H)
        w0 = pl.multiple_of((start - first_row) * GRID_W, GRID_W)
        return w0, grow - start

    def na_scores(r, s_ref, m_ref):
        w0, off = window(r)
        q0 = pl.multiple_of(r * GRID_W, GRID_W)
        for p in range(NA_PAIRS):
            ls = slice(p * 128, (p + 1) * 128)
            q = tok_ref[pl.ds(q0, GRID_W), _sub(TOK_QA, ls)]
            q2 = jnp.concatenate([jnp.where(low, q, jnp.zeros_like(q)),
                                  jnp.where(low, jnp.zeros_like(q), q)], axis=0)
            kw = kv_ref[pl.ds(w0, win), ls]
            s = lax.dot_general(q2, kw, _NT, preferred_element_type=f32)
            for g in range(128 // NA_GROUP):
                gs = slice(g * NA_GROUP, (g + 1) * NA_GROUP)
                m = None
                for j in range(win // 128):
                    js = slice(j * 128, (j + 1) * 128)
                    blk = s[gs, js] + bias_ref[p, 2 * j - off + (NA_WIN_H - 1), gs, :]
                    s_ref[p, gs, js] = blk
                    m = blk if m is None else jnp.maximum(m, blk)
                m_ref[p, gs, :] = jnp.broadcast_to(jnp.max(m, axis=-1, keepdims=True),
                                                   (NA_GROUP, 128))

    def na_probs(s_ref, m_ref, p_ref):
        for p in range(NA_PAIRS):
            for g in range(128 // NA_GROUP):
                gs = slice(g * NA_GROUP, (g + 1) * NA_GROUP)
                m = m_ref[p, gs, :]
                for j in range(win // 128):
                    js = slice(j * 128, (j + 1) * 128)
                    p_ref[p, gs, js] = jnp.exp(s_ref[p, gs, js] - m).astype(bf16)

    def na_output(r, p_ref):
        w0, _ = window(r)
        q0 = pl.multiple_of(r * GRID_W, GRID_W)
        for p in range(NA_PAIRS):
            ls = slice(p * 128, (p + 1) * 128)
            vw = jnp.concatenate(
                [kv_ref[pl.ds(w0, win), NA_WIDTH + p * 128:NA_WIDTH + (p + 1) * 128], ones], axis=1)
            o2 = jnp.dot(p_ref[p], vw, preferred_element_type=f32)
            o2 = o2[:, 0:128] / o2[:, 128:256]
            o = jnp.where(low, o2[0:GRID_W], o2[GRID_W:])
            gate = tok_ref[pl.ds(q0, GRID_W), _sub(TOK_GA, ls)].astype(f32)
            mix_ref[pl.ds(q0, GRID_W), ls] = (o * gate).astype(bf16)

    s_refs, m_refs, p_refs = (s0_ref, s1_ref), (m0_ref, m1_ref), (p0_ref, p1_ref)
    for t in range(n_rows + 2):
        if t < n_rows:
            na_scores(t, s_refs[t % 2], m_refs[t % 2])
        if 1 <= t <= n_rows:
            na_probs(s_refs[(t - 1) % 2], m_refs[(t - 1) % 2], p_refs[(t - 1) % 2])
        if t >= 2:
            na_output(t - 2, p_refs[t % 2])

    C = RET_CHUNK

    def ret_chunk(c, carry):
        t0 = pl.multiple_of(c * C, C)
        for h in range(RET_HEADS):
            hs = slice(h * RET_HEAD_DIM, (h + 1) * RET_HEAD_DIM)
            q = tok_ref[pl.ds(t0, C), _sub(TOK_QR, hs)]
            k = tok_ref[pl.ds(t0, C), _sub(TOK_KR, hs)]
            v = tok_ref[pl.ds(t0, C), _sub(TOK_VR, hs)]
            s = lax.dot_general(q, k, _NT, preferred_element_type=f32) * dmat_ref[h]
            o = jnp.dot(s.astype(bf16), v, preferred_element_type=f32)
            sf = state_ref[h]
            o = o + jnp.dot(q, sf.astype(bf16), preferred_element_type=f32) * dec_ref[DEC_QF, :, hs]
            o = o + jnp.dot(q, sb_ref[c, h], preferred_element_type=f32) * dec_ref[DEC_QB, :, hs]
            kd = (k.astype(f32) * dec_ref[DEC_KF, :, hs]).astype(bf16)
            state_ref[h] = sf * cdec_ref[0:1, hs] + lax.dot_general(
                kd, v, _TN, preferred_element_type=f32)
            ms = jnp.mean(o * o, axis=-1, keepdims=True)
            rn = o * lax.rsqrt(ms + NORM_EPS) * og_ref[:, hs]
            g = tok_ref[pl.ds(t0, C), _sub(TOK_GR, hs)].astype(f32)
            mix_ref[pl.ds(t0, C), NA_WIDTH + h * RET_HEAD_DIM:NA_WIDTH + (h + 1) * RET_HEAD_DIM] = (
                rn * g).astype(bf16)
        return carry

    lax.fori_loop(0, tt // C, ret_chunk, 0, unroll=True)

    out = jnp.dot(mix_ref[...], wo_ref[...], preferred_element_type=f32)
    y_ref[...] = x_ref[...] + mod_ref[2:3, :] * out


def _mix_out(x, mod3, tok_arr, kv, sb, bias, dmat, dec, cdec, og, w_out):
    B, T, _ = x.shape
    tt = MIX_TILE
    nt = T // tt
    C = RET_CHUNK
    rows = T // GRID_W
    assert T % tt == 0 and T >= tt + 2 * HALO_TOKENS, (T, tt)
    tok = lambda b, i: (b, i, 0)
    c2 = lambda b, i: (0, 0)
    halo_spec =pl.BlockSpec((None, pl.Element(tt + 2 * HALO_TOKENS), pl.Element(2 * NA_WIDTH)),
                             lambda b, i: (b, _halo_start(i, tt, T), 0))
    return pl.pallas_call(
        functools.partial(_mix_out_kernel, rows=rows),
        grid=(B, nt),
        in_specs=[pl.BlockSpec((None, tt, D_MODEL), tok),
                  pl.BlockSpec((None, 3, D_MODEL), lambda b, i: (b, 0, 0)),
                  pl.BlockSpec((None, tt, TOK_WIDTH), tok),
                  halo_spec,
                  pl.BlockSpec((NA_PAIRS, BIAS_SLOTS, 128, 128), lambda b, i: (0, 0, 0, 0),
                               pipeline_mode=pl.Buffered(1)),
                  pl.BlockSpec((None, tt // C, RET_HEADS, C, C), lambda b, i: (b, i, 0, 0, 0)),
                  pl.BlockSpec((RET_HEADS, C, C), lambda b, i: (0, 0, 0)),
                  pl.BlockSpec((DEC_TABLES, C, RET_WIDTH), lambda b, i: (0, 0, 0)),
                  pl.BlockSpec((8, RET_WIDTH), c2),
                  pl.BlockSpec((1, RET_WIDTH), c2),
                  pl.BlockSpec((D_MODEL, D_MODEL), c2, pipeline_mode=pl.Buffered(1))],
        out_specs=pl.BlockSpec((None, tt, D_MODEL), tok),
        out_shape=jax.ShapeDtypeStruct((B, T, D_MODEL), jnp.float32),
        scratch_shapes=[pltpu.VMEM((RET_HEADS, C, C), jnp.float32),
                        pltpu.VMEM((tt, D_MODEL), jnp.bfloat16),
                        pltpu.VMEM((NA_PAIRS, 128, NA_WIN_H * GRID_W), jnp.float32),
                        pltpu.VMEM((NA_PAIRS, 128, NA_WIN_H * GRID_W), jnp.float32),
                        pltpu.VMEM((NA_PAIRS, 128, 128), jnp.float32),
                        pltpu.VMEM((NA_PAIRS, 128, 128), jnp.float32),
                        pltpu.VMEM((NA_PAIRS, 128, NA_WIN_H * GRID_W), jnp.bfloat16),
                        pltpu.VMEM((NA_PAIRS, 128, NA_WIN_H * GRID_W), jnp.bfloat16)],
        compiler_params=pltpu.CompilerParams(dimension_semantics=("arbitrary", "arbitrary"),
                                             vmem_limit_bytes=VMEM_LIMIT_BYTES),
        name="mix_out",
    )(x, mod3, tok_arr, kv, bias, sb, dmat, dec, cdec, og, w_out)


def _rope_tables(positions):
    half = RET_HEAD_DIM // 2
    inv = ROPE_BASE ** (-jnp.arange(half, dtype=jnp.float32) / half)
    ang = positions.astype(jnp.float32)[:, None] * inv[None, :]
    cos, sin = jnp.cos(ang), jnp.sin(ang)
    return jnp.concatenate([cos, cos], axis=1), jnp.concatenate([sin, sin], axis=1)


def _layer(x, mod3, shared):
    (gain, w_in, bd, qg, kg, cosw, sinw, base, bias, dmat, dec, cdec, og, w_out) = shared
    nt = x.shape[1] // PROJ_TILE
    tok_arr, kv, sb = _in_proj(x, mod3, gain, w_in, bd, qg, kg, cosw, sinw, base[:nt], dec, cdec)
    return _mix_out(x, mod3, tok_arr, kv, sb, bias, dmat, dec, cdec, og, w_out)


def kernel(x_prompt, x_sample, c_prompt, c_sample, norm_gain, w_ada, b_ada, w_in, na_q_gain,
           na_k_gain, na_rpb, ret_decay_f, ret_decay_b, ret_out_gain, w_out):
    depth = norm_gain.shape[0]
    bp, bs = x_prompt.shape[0], x_sample.shape[0]
    nb = -(-(bp + bs) // 8) * 8
    t_max = max(x_prompt.shape[1], x_sample.shape[1])
    cosw, sinw = _rope_tables(jnp.arange(PROJ_TILE))
    base = jnp.stack(_rope_tables(jnp.arange(t_max // PROJ_TILE) * PROJ_TILE), axis=1)
    hid = np.arange(256) // NA_HEAD_DIM
    bd = jnp.asarray(hid[:, None] == hid[None, :], jnp.bfloat16)
    c_all = jnp.concatenate(
        [c_prompt, c_sample, jnp.zeros((nb - bp - bs, D_MODEL), jnp.float32)], axis=0)
    y_prompt, y_sample = x_prompt, x_sample
    for l in range(depth):
        rpb = jnp.pad(na_rpb[l], ((0, 0), (0, 0), (0, 128 - (2 * NA_WIN_W - 1))))
        mod, dmat, dec, cdec, bias = _prep(c_all, w_ada[l], b_ada[l][None],
                                           ret_decay_f[l], ret_decay_b[l], rpb)
        mod3 = mod.reshape(nb, 3, D_MODEL)
        shared = (norm_gain[l][None], w_in[l].astype(jnp.bfloat16), bd,
                  jnp.tile(na_q_gain[l], NA_HEADS)[None], jnp.tile(na_k_gain[l], NA_HEADS)[None],
                  cosw, sinw, base, bias, dmat, dec, cdec,
                  ret_out_gain[l].reshape(1, RET_WIDTH), w_out[l].astype(jnp.bfloat16))
        y_prompt = _layer(y_prompt, mod3[:bp], shared)
        y_sample = _layer(y_sample, mod3[bp:bp + bs], shared)
    return (y_prompt, y_sample)
```

```python
import functools

import numpy as np
import jax
import jax.numpy as jnp
from jax import lax
from jax.experimental import pallas as pl
from jax.experimental.pallas import tpu as pltpu

D_MODEL = 1024
GRID_W = 64
NA_HEADS = 8
NA_HEAD_DIM = 64
NA_WIDTH = NA_HEADS * NA_HEAD_DIM
NA_PAIRS = NA_HEADS // 2
NA_WIN_H = 8
NA_WIN_W = 16
RET_HEADS = 4
RET_HEAD_DIM = 128
RET_WIDTH = RET_HEADS * RET_HEAD_DIM
RET_CHUNK = 128
ROPE_BASE = 10000.0
NORM_EPS = 1e-6
IN_WIDTH = 4 * NA_WIDTH + 4 * RET_WIDTH
SEG = 512
LANES = 128

PROJ_TILE = 1024
MIX_TILE = 512
HALO_ROWS = NA_WIN_H // 2
HALO_TOKENS = HALO_ROWS * GRID_W
BIAS_SLOTS = 2 * NA_WIN_H - 2
NA_GROUP = 16

TOK_QA, TOK_GA, TOK_QR, TOK_KR, TOK_VR, TOK_GR = 0, 4, 8, 12, 16, 20
TOK_SLABS = 24
KV_KA, KV_VA = 0, NA_PAIRS
KV_SLABS = 2 * NA_PAIRS
DEC_QF, DEC_QB, DEC_KF, DEC_KB = range(4)
DEC_TABLES = 4
VMEM_LIMIT_BYTES = 56 * 1024 * 1024

_NT = (((1,), (1,)), ((), ()))
_TN = (((0,), (0,)), ((), ()))


def _silu(v):
    return v / (1.0 + jnp.exp(-v))


def _slab(v, n):
    return v[:, n * LANES:(n + 1) * LANES]


def _prep_kernel(dec_f_ref, dec_b_ref, c_ref, w_ref, b_ref, rpb_ref,
                 mod_ref, dmat_ref, dec_ref, cdec_ref, bias_ref):
    c = c_ref[...]
    mod_ref[...] = jnp.dot(_silu(c), w_ref[...], preferred_element_type=jnp.float32) + b_ref[...]

    @pl.when(pl.program_id(0) == 0)
    def _():
        C = RET_CHUNK
        pos = lax.broadcasted_iota(jnp.int32, (C, LANES), 0).astype(jnp.float32)
        ri = lax.broadcasted_iota(jnp.int32, (C, C), 0)
        ci = lax.broadcasted_iota(jnp.int32, (C, C), 1)
        diff = (ri - ci).astype(jnp.float32)
        for h in range(RET_HEADS):
            lgf = -jnp.exp(jnp.full((1, LANES), dec_f_ref[h], jnp.float32))
            lgb = -jnp.exp(jnp.full((1, LANES), dec_b_ref[h], jnp.float32))
            dec_ref[DEC_QF, h] = jnp.exp(lgf * (pos + 1.0))
            dec_ref[DEC_QB, h] = jnp.exp(lgb * (C - pos))
            dec_ref[DEC_KF, h] = jnp.exp(lgf * (C - 1.0 - pos))
            dec_ref[DEC_KB, h] = jnp.exp(lgb * pos)
            cdec_ref[0, h] = jnp.broadcast_to(jnp.exp(lgf * C), (8, LANES))
            cdec_ref[1, h] = jnp.broadcast_to(jnp.exp(lgb * C), (8, LANES))
            dmat_ref[h] = jnp.where(diff >= 0, jnp.exp(lgf * jnp.maximum(diff, 0.0)),
                                    jnp.exp(lgb * jnp.maximum(-diff, 0.0)))

        qcol = lax.broadcasted_iota(jnp.int32, (GRID_W, LANES), 0)
        lane = lax.broadcasted_iota(jnp.int32, (GRID_W, LANES), 1)
        kcol = lane % GRID_W
        wstart = jnp.clip(qcol - NA_WIN_W // 2, 0, GRID_W - NA_WIN_W)
        valid = (kcol >= wstart) & (kcol < wstart + NA_WIN_W)
        centre = LANES - (NA_WIN_W - 1)
        for h in range(NA_HEADS):
            for d in range(BIAS_SLOTS):
                even = pltpu.roll(jnp.broadcast_to(rpb_ref[h, d:d + 1, :], (GRID_W, LANES)),
                                  centre, 1, stride=1, stride_axis=0)
                odd = pltpu.roll(jnp.broadcast_to(rpb_ref[h, d + 1:d + 2, :], (GRID_W, LANES)),
                                 (centre + GRID_W) % LANES, 1, stride=1, stride_axis=0)
                blk = jnp.where(valid, jnp.where(lane < GRID_W, even, odd), -jnp.inf)
                bias_ref[h // 2, d, (h % 2) * GRID_W:(h % 2 + 1) * GRID_W, :] = blk


def _prep(c_all, w_ada, b_ada, dec_f, dec_b, rpb):
    nb = c_all.shape[0]
    C = RET_CHUNK
    f32 = jnp.float32
    smem = pl.BlockSpec(memory_space=pltpu.SMEM)
    const2 = lambda j: (0, 0)
    const3 = lambda j: (0, 0, 0)
    const4 = lambda j: (0, 0, 0, 0)
    return pl.pallas_call(
        _prep_kernel,
        grid=(3,),
        in_specs=[smem, smem,
                  pl.BlockSpec((nb, D_MODEL), const2),
                  pl.BlockSpec((D_MODEL, D_MODEL), lambda j: (0, j)),
                  pl.BlockSpec((1, D_MODEL), lambda j: (0, j)),
                  pl.BlockSpec((NA_HEADS, 2 * NA_WIN_H - 1, LANES), const3)],
        out_specs=[pl.BlockSpec((nb, D_MODEL), lambda j: (0, j)),
                   pl.BlockSpec((RET_HEADS, C, C), const3),
                   pl.BlockSpec((DEC_TABLES, RET_HEADS, C, LANES), const4),
                   pl.BlockSpec((2, RET_HEADS, 8, LANES), const4),
                   pl.BlockSpec((NA_PAIRS, BIAS_SLOTS, LANES, LANES), const4)],
        out_shape=[jax.ShapeDtypeStruct((nb, 3 * D_MODEL), f32),
                   jax.ShapeDtypeStruct((RET_HEADS, C, C), f32),
                   jax.ShapeDtypeStruct((DEC_TABLES, RET_HEADS, C, LANES), f32),
                   jax.ShapeDtypeStruct((2, RET_HEADS, 8, LANES), f32),
                   jax.ShapeDtypeStruct((NA_PAIRS, BIAS_SLOTS, LANES, LANES), f32)],
        compiler_params=pltpu.CompilerParams(dimension_semantics=("arbitrary",),
                                             vmem_limit_bytes=VMEM_LIMIT_BYTES),
        name="prep",
    )(dec_f, dec_b, c_all, w_ada, b_ada, rpb)


def _in_proj_kernel(x_ref, mod_ref, gain_ref, w_ref, bd_ref, qg_ref, kg_ref, cos_ref, sin_ref,
                    base_ref, dec_ref, cdec_ref,
                    tok_ref, kv_ref, sb_ref,
                    state_ref):
    bf16, f32 = jnp.bfloat16, jnp.float32
    tt = x_ref.shape[0]

    @pl.when(pl.program_id(1) == 0)
    def _():
        state_ref[...] = jnp.zeros_like(state_ref)

    x = x_ref[...]
    ms = jnp.mean(x * x, axis=-1, keepdims=True)
    a = gain_ref[...] * (1.0 + mod_ref[1:2, :])
    hb = (x * lax.rsqrt(ms + NORM_EPS) * a + mod_ref[0:1, :]).astype(bf16)

    def seg(s):
        return jnp.dot(hb, w_ref[s], preferred_element_type=f32)

    def head_norm(p, g):
        pp = (p * p).astype(bf16)
        ss = jnp.concatenate(
            [jnp.dot(pp[:, c * 256:(c + 1) * 256], bd_ref[...], preferred_element_type=f32)
             for c in range(NA_WIDTH // 256)], axis=1)
        return p * lax.rsqrt(ss * (1.0 / NA_HEAD_DIM) + NORM_EPS) * g

    def put(ref, first, val):
        for n in range(val.shape[1] // LANES):
            ref[first + n] = _slab(val, n)

    put(tok_ref, TOK_QA, head_norm(seg(0), qg_ref[...] * (NA_HEAD_DIM ** -0.5)).astype(bf16))
    put(kv_ref, KV_KA, head_norm(seg(1), kg_ref[...]).astype(bf16))
    put(kv_ref, KV_VA, seg(2).astype(bf16))
    put(tok_ref, TOK_GA, _silu(seg(3)).astype(bf16))

    c0, s0 = base_ref[0:1, :], base_ref[1:2, :]
    cw, sw = cos_ref[...], sin_ref[...]
    hlane = lax.broadcasted_iota(jnp.int32, (1, RET_HEAD_DIM), 1)
    cosf = c0 * cw - s0 * sw
    sinf = (s0 * cw + c0 * sw) * jnp.where(hlane < RET_HEAD_DIM // 2, -1.0, 1.0)

    def rotary(p):
        outs = []
        for h in range(RET_HEADS):
            ph = _slab(p, h)
            outs.append(ph * cosf + pltpu.roll(ph, RET_HEAD_DIM // 2, 1) * sinf)
        return jnp.concatenate(outs, axis=1)

    put(tok_ref, TOK_QR, rotary(seg(4)).astype(bf16))
    kr = rotary(seg(5)) * (RET_HEAD_DIM ** -0.5)
    put(tok_ref, TOK_KR, kr.astype(bf16))
    vb = seg(6).astype(bf16)
    put(tok_ref, TOK_VR, vb)
    put(tok_ref, TOK_GR, _silu(seg(7)).astype(bf16))

    C = RET_CHUNK
    for c in reversed(range(tt // C)):
        rows = slice(c * C, (c + 1) * C)
        for h in range(RET_HEADS):
            s_old = state_ref[h]
            sb_ref[c, h] = s_old.astype(bf16)
            kd = (_slab(kr[rows], h) * dec_ref[DEC_KB, h]).astype(bf16)
            kv = lax.dot_general(kd, _slab(vb[rows], h), _TN, preferred_element_type=f32)
            state_ref[h] = s_old * cdec_ref[1, h, 0:1, :] + kv


def _in_proj(x, mod3, gain, w_in, bd, qg, kg, cosw, sinw, base, dec, cdec):
    B, T, _ = x.shape
    tt = PROJ_TILE
    nt = T // tt
    C = RET_CHUNK
    rev = lambda b, i: (b, nt - 1 - i, 0)
    rev4 = lambda b, i: (b, 0, nt - 1 - i, 0)
    c2 = lambda b, i: (0, 0)
    c3 = lambda b, i: (0, 0, 0)
    c4 = lambda b, i: (0, 0, 0, 0)
    return pl.pallas_call(
        _in_proj_kernel,
        grid=(B, nt),
        in_specs=[pl.BlockSpec((None, tt, D_MODEL), rev),
                  pl.BlockSpec((None, 3, D_MODEL), lambda b, i: (b, 0, 0)),
                  pl.BlockSpec((1, D_MODEL), c2),
                  pl.BlockSpec((IN_WIDTH // SEG, D_MODEL, SEG), c3, pipeline_mode=pl.Buffered(1)),
                  pl.BlockSpec((256, 256), c2),
                  pl.BlockSpec((1, NA_WIDTH), c2),
                  pl.BlockSpec((1, NA_WIDTH), c2),
                  pl.BlockSpec((tt, RET_HEAD_DIM), c2),
                  pl.BlockSpec((tt, RET_HEAD_DIM), c2),
                  pl.BlockSpec((None, 2, RET_HEAD_DIM), lambda b, i: (nt - 1 - i, 0, 0)),
                  pl.BlockSpec((DEC_TABLES, RET_HEADS, C, LANES), c4),
                  pl.BlockSpec((2, RET_HEADS, 8, LANES), c4)],
        out_specs=[pl.BlockSpec((None, TOK_SLABS, tt, LANES), rev4),
                   pl.BlockSpec((None, KV_SLABS, tt, LANES), rev4),
                   pl.BlockSpec((None, tt // C, RET_HEADS, C, C),
                                lambda b, i: (b, nt - 1 - i, 0, 0, 0))],
        out_shape=[jax.ShapeDtypeStruct((B, TOK_SLABS, T, LANES), jnp.bfloat16),
                   jax.ShapeDtypeStruct((B, KV_SLABS, T, LANES), jnp.bfloat16),
                   jax.ShapeDtypeStruct((B, T // C, RET_HEADS, C, C), jnp.bfloat16)],
        scratch_shapes=[pltpu.VMEM((RET_HEADS, C, C), jnp.float32)],
        compiler_params=pltpu.CompilerParams(dimension_semantics=("arbitrary", "arbitrary"),
                                             vmem_limit_bytes=VMEM_LIMIT_BYTES),
        name="in_proj",
    )(x, mod3, gain, w_in, bd, qg, kg, cosw, sinw, base, dec, cdec)


def _halo_start(i, tt, seq_len):
    units = jnp.clip(i * (tt // HALO_TOKENS) - 1, 0, (seq_len - tt) // HALO_TOKENS - 2)
    return units * HALO_TOKENS


def _mix_out_kernel(x_ref, mod_ref, tok_ref, kv_ref, bias_ref, sb_ref,
                    dmat_ref, dec_ref, cdec_ref, og_ref, wo_ref,
                    y_ref,
                    state_ref, mix_ref, s0_ref, s1_ref, m0_ref, m1_ref,
                    p0_ref, p1_ref, *, rows):
    bf16, f32 = jnp.bfloat16, jnp.float32
    tt = x_ref.shape[0]
    n_rows = tt // GRID_W
    i = pl.program_id(1)

    @pl.when(i == 0)
    def _():
        state_ref[...] = jnp.zeros_like(state_ref)

    first_row = _halo_start(i, tt, rows * GRID_W) // GRID_W

    lane = lax.broadcasted_iota(jnp.int32, (GRID_W, LANES), 1)
    low = lane < NA_HEAD_DIM
    win = NA_WIN_H * GRID_W
    n_blk = win // LANES
    ones = jnp.ones((win, LANES), bf16)

    def window(r):
        grow = i * n_rows + r
        start = jnp.clip(grow - NA_WIN_H // 2, 0, rows - NA_WIN_H)
        w0 = pl.multiple_of((start - first_row) * GRID_W, GRID_W)
        return w0, grow - start

    def na_scores(r, s_ref, m_ref):
        w0, off = window(r)
        q0 = pl.multiple_of(r * GRID_W, GRID_W)
        for p in range(NA_PAIRS):
            q = tok_ref[TOK_QA + p, pl.ds(q0, GRID_W), :]
            q2 = jnp.concatenate([jnp.where(low, q, jnp.zeros_like(q)),
                                  jnp.where(low, jnp.zeros_like(q), q)], axis=0)
            kw = kv_ref[KV_KA + p, pl.ds(w0, win), :]
            s = lax.dot_general(q2, kw, _NT, preferred_element_type=f32)
            for g in range(LANES // NA_GROUP):
                gs = slice(g * NA_GROUP, (g + 1) * NA_GROUP)
                m = None
                for j in range(n_blk):
                    blk = _slab(s[gs], j) + bias_ref[p, 2 * j - off + (NA_WIN_H - 1), gs, :]
                    s_ref[p, j, gs, :] = blk
                    m = blk if m is None else jnp.maximum(m, blk)
                m_ref[p, gs, :] = jnp.broadcast_to(jnp.max(m, axis=-1, keepdims=True),
                                                   (NA_GROUP, LANES))

    def na_probs(s_ref, m_ref, p_ref):
        for p in range(NA_PAIRS):
            for g in range(LANES // NA_GROUP):
                gs = slice(g * NA_GROUP, (g + 1) * NA_GROUP)
                m = m_ref[p, gs, :]
                for j in range(n_blk):
                    p_ref[p, j, gs, :] = jnp.exp(s_ref[p, j, gs, :] - m).astype(bf16)

    def na_output(r, p_ref):
        w0, _ = window(r)
        q0 = pl.multiple_of(r * GRID_W, GRID_W)
        for p in range(NA_PAIRS):
            vw = jnp.concatenate([kv_ref[KV_VA + p, pl.ds(w0, win), :], ones], axis=1)
            pm = jnp.concatenate([p_ref[p, j] for j in range(n_blk)], axis=1)
            o2 = jnp.dot(pm, vw, preferred_element_type=f32)
            o2 = _slab(o2, 0) / _slab(o2, 1)
            o = jnp.where(low, o2[0:GRID_W], o2[GRID_W:])
            gate = tok_ref[TOK_GA + p, pl.ds(q0, GRID_W), :].astype(f32)
            mix_ref[p, pl.ds(q0, GRID_W), :] = (o * gate).astype(bf16)

    s_refs, m_refs, p_refs = (s0_ref, s1_ref), (m0_ref, m1_ref), (p0_ref, p1_ref)
    for t in range(n_rows + 2):
        if t < n_rows:
            na_scores(t, s_refs[t % 2], m_refs[t % 2])
        if 1 <= t <= n_rows:
            na_probs(s_refs[(t - 1) % 2], m_refs[(t - 1) % 2], p_refs[(t - 1) % 2])
        if t >= 2:
            na_output(t - 2, p_refs[t % 2])

    C = RET_CHUNK
    for c in range(tt // C):
        rws = slice(c * C, (c + 1) * C)
        for h in range(RET_HEADS):
            q = tok_ref[TOK_QR + h, rws, :]
            k = tok_ref[TOK_KR + h, rws, :]
            v = tok_ref[TOK_VR + h, rws, :]
            s = lax.dot_general(q, k, _NT, preferred_element_type=f32) * dmat_ref[h]
            o = jnp.dot(s.astype(bf16), v, preferred_element_type=f32)
            sf = state_ref[h]
            o = o + jnp.dot(q, sf.astype(bf16), preferred_element_type=f32) * dec_ref[DEC_QF, h]
            o = o + jnp.dot(q, sb_ref[c, h], preferred_element_type=f32) * dec_ref[DEC_QB, h]
            kd = (k.astype(f32) * dec_ref[DEC_KF, h]).astype(bf16)
            state_ref[h] = sf * cdec_ref[0, h, 0:1, :] + lax.dot_general(
                kd, v, _TN, preferred_element_type=f32)
            ms = jnp.mean(o * o, axis=-1, keepdims=True)
            rn = o * lax.rsqrt(ms + NORM_EPS) * og_ref[h]
            g = tok_ref[TOK_GR + h, rws, :].astype(f32)
            mix_ref[NA_PAIRS + h, rws, :] = (rn * g).astype(bf16)

    mix = jnp.concatenate([mix_ref[n] for n in range(NA_PAIRS + RET_HEADS)], axis=1)
    out = jnp.dot(mix, wo_ref[...], preferred_element_type=f32)
    y_ref[...] = x_ref[...] + mod_ref[2:3, :] * out


def _mix_out(x, mod3, tok_arr, kv, sb, bias, dmat, dec, cdec, og, w_out):
    B, T, _ = x.shape
    tt = MIX_TILE
    nt = T // tt
    C = RET_CHUNK
    rows = T // GRID_W
    win = NA_WIN_H * GRID_W
    assert T % tt == 0 and T >= tt + 2 * HALO_TOKENS, (T, tt)
    tok = lambda b, i: (b, i, 0)
    c2 = lambda b, i: (0, 0)
    c3 = lambda b, i: (0, 0, 0)
    c4 = lambda b, i: (0, 0, 0, 0)
    halo_spec = pl.BlockSpec(
        (None, pl.Element(KV_SLABS), pl.Element(tt + 2 * HALO_TOKENS), pl.Element(LANES)),
        lambda b, i: (b, 0, _halo_start(i, tt, T), 0))
    score_shape = (NA_PAIRS, win // LANES, LANES, LANES)
    return pl.pallas_call(
        functools.partial(_mix_out_kernel, rows=rows),
        grid=(B, nt),
        in_specs=[pl.BlockSpec((None, tt, D_MODEL), tok),
                  pl.BlockSpec((None, 3, D_MODEL), lambda b, i: (b, 0, 0)),
                  pl.BlockSpec((None, TOK_SLABS, tt, LANES), lambda b, i: (b, 0, i, 0)),
                  halo_spec,
                  pl.BlockSpec((NA_PAIRS, BIAS_SLOTS, LANES, LANES), c4,
                               pipeline_mode=pl.Buffered(1)),
                  pl.BlockSpec((None, tt // C, RET_HEADS, C, C), lambda b, i: (b, i, 0, 0, 0)),
                  pl.BlockSpec((RET_HEADS, C, C), c3),
                  pl.BlockSpec((DEC_TABLES, RET_HEADS, C, LANES), c4),
                  pl.BlockSpec((2, RET_HEADS, 8, LANES), c4),
                  pl.BlockSpec((RET_HEADS, 1, RET_HEAD_DIM), c3),
                  pl.BlockSpec((D_MODEL, D_MODEL), c2, pipeline_mode=pl.Buffered(1))],
        out_specs=pl.BlockSpec((None, tt, D_MODEL), tok),
        out_shape=jax.ShapeDtypeStruct((B, T, D_MODEL), jnp.float32),
        scratch_shapes=[pltpu.VMEM((RET_HEADS, C, C), jnp.float32),
                        pltpu.VMEM((NA_PAIRS + RET_HEADS, tt, LANES), jnp.bfloat16),
                        pltpu.VMEM(score_shape, jnp.float32),
                        pltpu.VMEM(score_shape, jnp.float32),
                        pltpu.VMEM((NA_PAIRS, LANES, LANES), jnp.float32),
                        pltpu.VMEM((NA_PAIRS, LANES, LANES), jnp.float32),
                        pltpu.VMEM(score_shape, jnp.bfloat16),
                        pltpu.VMEM(score_shape, jnp.bfloat16)],
        compiler_params=pltpu.CompilerParams(dimension_semantics=("arbitrary", "arbitrary"),
                                             vmem_limit_bytes=VMEM_LIMIT_BYTES),
        name="mix_out",
    )(x, mod3, tok_arr, kv, bias, sb, dmat, dec, cdec, og, w_out)


def _rope_tables(positions):
    half = RET_HEAD_DIM // 2
    inv = ROPE_BASE ** (-jnp.arange(half, dtype=jnp.float32) / half)
    ang = positions.astype(jnp.float32)[:, None] * inv[None, :]
    cos, sin = jnp.cos(ang), jnp.sin(ang)
    return jnp.concatenate([cos, cos], axis=1), jnp.concatenate([sin, sin], axis=1)


def _layer(x, mod3, shared):
    (gain, w_in, bd, qg, kg, cosw, sinw, base, bias, dmat, dec, cdec, og, w_out) = shared
    nt = x.shape[1] // PROJ_TILE
    tok_arr, kv, sb = _in_proj(x, mod3, gain, w_in, bd, qg, kg, cosw, sinw, base[:nt], dec, cdec)
    return _mix_out(x, mod3, tok_arr, kv, sb, bias, dmat, dec, cdec, og, w_out)


def kernel(x_prompt, x_sample, c_prompt, c_sample, norm_gain, w_ada, b_ada, w_in, na_q_gain,
           na_k_gain, na_rpb, ret_decay_f, ret_decay_b, ret_out_gain, w_out):
    depth = norm_gain.shape[0]
    bp, bs = x_prompt.shape[0], x_sample.shape[0]
    nb = -(-(bp + bs) // 8) * 8
    t_max = max(x_prompt.shape[1], x_sample.shape[1])
    cosw, sinw = _rope_tables(jnp.arange(PROJ_TILE))
    base = jnp.stack(_rope_tables(jnp.arange(t_max // PROJ_TILE) * PROJ_TILE), axis=1)
    hid = np.arange(256) // NA_HEAD_DIM
    bd = jnp.asarray(hid[:, None] == hid[None, :], jnp.bfloat16)
    c_all = jnp.concatenate(
        [c_prompt, c_sample, jnp.zeros((nb - bp - bs, D_MODEL), jnp.float32)], axis=0)
    y_prompt, y_sample = x_prompt, x_sample
    for l in range(depth):
        rpb = jnp.pad(na_rpb[l], ((0, 0), (0, 0), (0, LANES - (2 * NA_WIN_W - 1))))
        mod, dmat, dec, cdec, bias = _prep(c_all, w_ada[l], b_ada[l][None],
                                           ret_decay_f[l], ret_decay_b[l], rpb)
        mod3 = mod.reshape(nb, 3, D_MODEL)
        w_seg = w_in[l].astype(jnp.bfloat16).reshape(D_MODEL, IN_WIDTH // SEG, SEG)
        shared = (norm_gain[l][None], w_seg.transpose(1, 0, 2), bd,
                  jnp.tile(na_q_gain[l], NA_HEADS)[None], jnp.tile(na_k_gain[l], NA_HEADS)[None],
                  cosw, sinw, base, bias, dmat, dec, cdec,
                  ret_out_gain[l].reshape(RET_HEADS, 1, RET_HEAD_DIM),
                  w_out[l].astype(jnp.bfloat16))
        y_prompt = _layer(y_prompt, mod3[:bp], shared)
        y_sample = _layer(y_sample, mod3[bp:bp + bs], shared)
    return (y_prompt, y_sample)
```

```python
import functools

import numpy as np
import jax
import jax.numpy as jnp
from jax import lax
from jax.experimental import pallas as pl
from jax.experimental.pallas import tpu as pltpu

D_MODEL = 1024
GRID_W = 64
NA_HEADS = 8
NA_HEAD_DIM = 64
NA_WIDTH = NA_HEADS * NA_HEAD_DIM
NA_PAIRS = NA_HEADS // 2
NA_WIN_H = 8
NA_WIN_W = 16
RET_HEADS = 4
RET_HEAD_DIM = 128
RET_WIDTH = RET_HEADS * RET_HEAD_DIM
RET_CHUNK = 128
ROPE_BASE = 10000.0
NORM_EPS = 1e-6
IN_WIDTH = 4 * NA_WIDTH + 4 * RET_WIDTH
SEG = 512
LANES = 128

PROJ_TILE = 1024
MIX_TILE = 512
HALO_ROWS = NA_WIN_H // 2
HALO_TOKENS = HALO_ROWS * GRID_W
BIAS_SLOTS = 2 * NA_WIN_H - 2
NA_GROUP = 16

TOK_QA, TOK_GA, TOK_QR, TOK_KR, TOK_VR, TOK_GR = 0, 4, 8, 12, 16, 20
TOK_SLABS = 24
KV_KA, KV_VA = 0, NA_PAIRS
KV_SLABS = 2 * NA_PAIRS
DEC_QF, DEC_QB, DEC_KF, DEC_KB = range(4)
DEC_TABLES = 4
VMEM_LIMIT_BYTES = 56 * 1024 * 1024

_NT = (((1,), (1,)), ((), ()))
_TN = (((0,), (0,)), ((), ()))


def _silu(v):
    return v / (1.0 + jnp.exp(-v))


def _slab(v, n):
    return v[:, n * LANES:(n + 1) * LANES]


def _prep_kernel(dec_f_ref, dec_b_ref, c_ref, w_ref, b_ref, rpb_ref,
                 mod_ref, dmat_ref, dec_ref, cdec_ref, bias_ref):
    c = c_ref[...]
    mod_ref[...] = jnp.dot(_silu(c), w_ref[...], preferred_element_type=jnp.float32) + b_ref[...]

    @pl.when(pl.program_id(0) == 0)
    def _():
        C = RET_CHUNK
        pos = lax.broadcasted_iota(jnp.int32, (C, LANES), 0).astype(jnp.float32)
        ri = lax.broadcasted_iota(jnp.int32, (C, C), 0)
        ci = lax.broadcasted_iota(jnp.int32, (C, C), 1)
        diff = (ri - ci).astype(jnp.float32)
        for h in range(RET_HEADS):
            lgf = -jnp.exp(jnp.full((1, LANES), dec_f_ref[h], jnp.float32))
            lgb = -jnp.exp(jnp.full((1, LANES), dec_b_ref[h], jnp.float32))
            dec_ref[DEC_QF, h] = jnp.exp(lgf * (pos + 1.0))
            dec_ref[DEC_QB, h] = jnp.exp(lgb * (C - pos))
            dec_ref[DEC_KF, h] = jnp.exp(lgf * (C - 1.0 - pos))
            dec_ref[DEC_KB, h] = jnp.exp(lgb * pos)
            cdec_ref[0, h] = jnp.broadcast_to(jnp.exp(lgf * C), (8, LANES))
            cdec_ref[1, h] = jnp.broadcast_to(jnp.exp(lgb * C), (8, LANES))
            dmat_ref[h] = jnp.where(diff >= 0, jnp.exp(lgf * jnp.maximum(diff, 0.0)),
                                    jnp.exp(lgb * jnp.maximum(-diff, 0.0)))

        qcol = lax.broadcasted_iota(jnp.int32, (GRID_W, LANES), 0)
        lane = lax.broadcasted_iota(jnp.int32, (GRID_W, LANES), 1)
        kcol = lane % GRID_W
        wstart = jnp.clip(qcol - NA_WIN_W // 2, 0, GRID_W - NA_WIN_W)
        valid = (kcol >= wstart) & (kcol < wstart + NA_WIN_W)
        centre = LANES - (NA_WIN_W - 1)
        for h in range(NA_HEADS):
            for d in range(BIAS_SLOTS):
                even = pltpu.roll(jnp.broadcast_to(rpb_ref[h, d:d + 1, :], (GRID_W, LANES)),
                                  centre, 1, stride=1, stride_axis=0)
                odd = pltpu.roll(jnp.broadcast_to(rpb_ref[h, d + 1:d + 2, :], (GRID_W, LANES)),
                                 (centre + GRID_W) % LANES, 1, stride=1, stride_axis=0)
                blk = jnp.where(valid, jnp.where(lane < GRID_W, even, odd), -jnp.inf)
                bias_ref[h // 2, d, (h % 2) * GRID_W:(h % 2 + 1) * GRID_W, :] = blk


def _prep(c_all, w_ada, b_ada, dec_f, dec_b, rpb):
    nb = c_all.shape[0]
    C = RET_CHUNK
    f32 = jnp.float32
    smem = pl.BlockSpec(memory_space=pltpu.SMEM)
    const2 = lambda j: (0, 0)
    const3 = lambda j: (0, 0, 0)
    const4 = lambda j: (0, 0, 0, 0)
    return pl.pallas_call(
        _prep_kernel,
        grid=(3,),
        in_specs=[smem, smem,
                  pl.BlockSpec((nb, D_MODEL), const2),
                  pl.BlockSpec((D_MODEL, D_MODEL), lambda j: (0, j)),
                  pl.BlockSpec((1, D_MODEL), lambda j: (0, j)),
                  pl.BlockSpec((NA_HEADS, 2 * NA_WIN_H - 1, LANES), const3)],
        out_specs=[pl.BlockSpec((nb, D_MODEL), lambda j: (0, j)),
                   pl.BlockSpec((RET_HEADS, C, C), const3),
                   pl.BlockSpec((DEC_TABLES, RET_HEADS, C, LANES), const4),
                   pl.BlockSpec((2, RET_HEADS, 8, LANES), const4),
                   pl.BlockSpec((NA_PAIRS, BIAS_SLOTS, LANES, LANES), const4)],
        out_shape=[jax.ShapeDtypeStruct((nb, 3 * D_MODEL), f32),
                   jax.ShapeDtypeStruct((RET_HEADS, C, C), f32),
                   jax.ShapeDtypeStruct((DEC_TABLES, RET_HEADS, C, LANES), f32),
                   jax.ShapeDtypeStruct((2, RET_HEADS, 8, LANES), f32),
                   jax.ShapeDtypeStruct((NA_PAIRS, BIAS_SLOTS, LANES, LANES), f32)],
        compiler_params=pltpu.CompilerParams(dimension_semantics=("arbitrary",),
                                             vmem_limit_bytes=VMEM_LIMIT_BYTES),
        name="prep",
    )(dec_f, dec_b, c_all, w_ada, b_ada, rpb)


def _in_proj_kernel(x_ref, mod_ref, gain_ref, w_ref, bd_ref, qg_ref, kg_ref, cos_ref, sin_ref,
                    base_ref, dec_ref, cdec_ref,
                    tok_ref, kv_ref, sb_ref,
                    state_ref):
    bf16, f32 = jnp.bfloat16, jnp.float32
    tt = x_ref.shape[0]

    @pl.when(pl.program_id(1) == 0)
    def _():
        state_ref[...] = jnp.zeros_like(state_ref)

    x = x_ref[...]
    ms = jnp.mean(x * x, axis=-1, keepdims=True)
    a = gain_ref[...] * (1.0 + mod_ref[1:2, :])
    hb = (x * lax.rsqrt(ms + NORM_EPS) * a + mod_ref[0:1, :]).astype(bf16)

    def seg(s):
        return jnp.dot(hb, w_ref[s], preferred_element_type=f32)

    def head_norm(p, g):
        pp = (p * p).astype(bf16)
        ss = jnp.concatenate(
            [jnp.dot(pp[:, c * 256:(c + 1) * 256], bd_ref[...], preferred_element_type=f32)
             for c in range(NA_WIDTH // 256)], axis=1)
        return p * lax.rsqrt(ss * (1.0 / NA_HEAD_DIM) + NORM_EPS) * g

    def put(ref, first, val):
        for n in range(val.shape[1] // LANES):
            ref[first + n] = _slab(val, n)

    put(tok_ref, TOK_QA, head_norm(seg(0), qg_ref[...] * (NA_HEAD_DIM ** -0.5)).astype(bf16))
    put(kv_ref, KV_KA, head_norm(seg(1), kg_ref[...]).astype(bf16))
    put(kv_ref, KV_VA, seg(2).astype(bf16))
    put(tok_ref, TOK_GA, _silu(seg(3)).astype(bf16))

    c0, s0 = base_ref[0:1, :], base_ref[1:2, :]
    cw, sw = cos_ref[...], sin_ref[...]
    hlane = lax.broadcasted_iota(jnp.int32, (1, RET_HEAD_DIM), 1)
    cosf = c0 * cw - s0 * sw
    sinf = (s0 * cw + c0 * sw) * jnp.where(hlane < RET_HEAD_DIM // 2, -1.0, 1.0)

    def rotary(p):
        outs = []
        for h in range(RET_HEADS):
            ph = _slab(p, h)
            outs.append(ph * cosf + pltpu.roll(ph, RET_HEAD_DIM // 2, 1) * sinf)
        return jnp.concatenate(outs, axis=1)

    put(tok_ref, TOK_QR, rotary(seg(4)).astype(bf16))
    kr = rotary(seg(5)) * (RET_HEAD_DIM ** -0.5)
    put(tok_ref, TOK_KR, kr.astype(bf16))
    vb = seg(6).astype(bf16)
    put(tok_ref, TOK_VR, vb)
    put(tok_ref, TOK_GR, _silu(seg(7)).astype(bf16))

    C = RET_CHUNK
    for c in reversed(range(tt // C)):
        rows = slice(c * C, (c + 1) * C)
        for h in range(RET_HEADS):
            s_old = state_ref[h]
            sb_ref[c, h] = s_old.astype(bf16)
            kd = (_slab(kr[rows], h) * dec_ref[DEC_KB, h]).astype(bf16)
            kv = lax.dot_general(kd, _slab(vb[rows], h), _TN, preferred_element_type=f32)
            state_ref[h] = s_old * cdec_ref[1, h, 0:1, :] + kv


def _in_proj(x, mod3, gain, w_in, bd, qg, kg, cosw, sinw, base, dec, cdec):
    B, T, _ = x.shape
    tt = PROJ_TILE
    nt = T // tt
    C = RET_CHUNK
    rev = lambda b, i: (b, nt - 1 - i, 0)
    rev4 = lambda b, i: (b, 0, nt - 1 - i, 0)
    c2 = lambda b, i: (0, 0)
    c3 = lambda b, i: (0, 0, 0)
    c4 = lambda b, i: (0, 0, 0, 0)
    return pl.pallas_call(
        _in_proj_kernel,
        grid=(B, nt),
        in_specs=[pl.BlockSpec((None, tt, D_MODEL), rev),
                  pl.BlockSpec((None, 3, D_MODEL), lambda b, i: (b, 0, 0)),
                  pl.BlockSpec((1, D_MODEL), c2),
                  pl.BlockSpec((IN_WIDTH // SEG, D_MODEL, SEG), c3, pipeline_mode=pl.Buffered(1)),
                  pl.BlockSpec((256, 256), c2),
                  pl.BlockSpec((1, NA_WIDTH), c2),
                  pl.BlockSpec((1, NA_WIDTH), c2),
                  pl.BlockSpec((tt, RET_HEAD_DIM), c2),
                  pl.BlockSpec((tt, RET_HEAD_DIM), c2),
                  pl.BlockSpec((None, 2, RET_HEAD_DIM), lambda b, i: (nt - 1 - i, 0, 0)),
                  pl.BlockSpec((DEC_TABLES, RET_HEADS, C, LANES), c4),
                  pl.BlockSpec((2, RET_HEADS, 8, LANES), c4)],
        out_specs=[pl.BlockSpec((None, TOK_SLABS, tt, LANES), rev4),
                   pl.BlockSpec((None, KV_SLABS, tt, LANES), rev4),
                   pl.BlockSpec((None, tt // C, RET_HEADS, C, C),
                                lambda b, i: (b, nt - 1 - i, 0, 0, 0))],
        out_shape=[jax.ShapeDtypeStruct((B, TOK_SLABS, T, LANES), jnp.bfloat16),
                   jax.ShapeDtypeStruct((B, KV_SLABS, T, LANES), jnp.bfloat16),
                   jax.ShapeDtypeStruct((B, T // C, RET_HEADS, C, C), jnp.bfloat16)],
        scratch_shapes=[pltpu.VMEM((RET_HEADS, C, C), jnp.float32)],
        compiler_params=pltpu.CompilerParams(dimension_semantics=("arbitrary", "arbitrary"),
                                             vmem_limit_bytes=VMEM_LIMIT_BYTES),
        name="in_proj",
    )(x, mod3, gain, w_in, bd, qg, kg, cosw, sinw, base, dec, cdec)


def _halo_start(i, tt, seq_len):
    units = jnp.clip(i * (tt // HALO_TOKENS) - 1, 0, (seq_len - tt) // HALO_TOKENS - 2)
    return units * HALO_TOKENS


def _mix_out_kernel(x_ref, mod_ref, tok_ref, kv_ref, bias_ref, sb_ref,
                    dmat_ref, dec_ref, cdec_ref, og_ref, wo_ref,
                    y_ref,
                    state_ref, mix_ref, prev_ref, s0_ref, s1_ref, m0_ref, m1_ref,
                    p0_ref, p1_ref, *, rows):
    nt = pl.num_programs(1) - 1
    i = pl.program_id(1)
    attend = functools.partial(
        _attend, tok_ref, kv_ref, bias_ref, sb_ref, dmat_ref, dec_ref, cdec_ref, og_ref,
        state_ref, mix_ref, prev_ref, (s0_ref, s1_ref), (m0_ref, m1_ref), (p0_ref, p1_ref),
        i=i, rows=rows, tt=x_ref.shape[0])
    project = functools.partial(_project, x_ref, mod_ref, wo_ref, y_ref, prev_ref)

    @pl.when(i == 0)
    def _():
        state_ref[...] = jnp.zeros_like(state_ref)
        attend()

    @pl.when((i > 0) & (i < nt))
    def _():
        project()
        attend()

    @pl.when(i == nt)
    def _():
        project()


def _project(x_ref, mod_ref, wo_ref, y_ref, prev_ref):
    mix = jnp.concatenate([prev_ref[n] for n in range(NA_PAIRS + RET_HEADS)], axis=1)
    out = jnp.dot(mix, wo_ref[...], preferred_element_type=jnp.float32)
    y_ref[...] = x_ref[...] + mod_ref[2:3, :] * out


def _attend(tok_ref, kv_ref, bias_ref, sb_ref, dmat_ref, dec_ref, cdec_ref, og_ref,
            state_ref, mix_ref, prev_ref, s_refs, m_refs, p_refs, *, i, rows, tt):
    bf16, f32 = jnp.bfloat16, jnp.float32
    n_rows = tt // GRID_W

    first_row = _halo_start(i, tt, rows * GRID_W) // GRID_W

    lane = lax.broadcasted_iota(jnp.int32, (GRID_W, LANES), 1)
    low = lane < NA_HEAD_DIM
    win = NA_WIN_H * GRID_W
    n_blk = win // LANES
    ones = jnp.ones((win, LANES), bf16)

    def window(r):
        grow = i * n_rows + r
        start = jnp.clip(grow - NA_WIN_H // 2, 0, rows - NA_WIN_H)
        w0 = pl.multiple_of((start - first_row) * GRID_W, GRID_W)
        return w0, grow - start

    def na_scores(r, s_ref, m_ref):
        w0, off = window(r)
        q0 = pl.multiple_of(r * GRID_W, GRID_W)
        for p in range(NA_PAIRS):
            q = tok_ref[TOK_QA + p, pl.ds(q0, GRID_W), :]
            q2 = jnp.concatenate([jnp.where(low, q, jnp.zeros_like(q)),
                                  jnp.where(low, jnp.zeros_like(q), q)], axis=0)
            kw = kv_ref[KV_KA + p, pl.ds(w0, win), :]
            s = lax.dot_general(q2, kw, _NT, preferred_element_type=f32)
            for g in range(LANES // NA_GROUP):
                gs = slice(g * NA_GROUP, (g + 1) * NA_GROUP)
                m = None
                for j in range(n_blk):
                    blk = _slab(s[gs], j) + bias_ref[p, 2 * j - off + (NA_WIN_H - 1), gs, :]
                    s_ref[p, j, gs, :] = blk
                    m = blk if m is None else jnp.maximum(m, blk)
                m_ref[p, gs, :] = jnp.broadcast_to(jnp.max(m, axis=-1, keepdims=True),
                                                   (NA_GROUP, LANES))

    def na_probs(s_ref, m_ref, p_ref):
        for p in range(NA_PAIRS):
            for g in range(LANES // NA_GROUP):
                gs = slice(g * NA_GROUP, (g + 1) * NA_GROUP)
                m = m_ref[p, gs, :]
                for j in range(n_blk):
                    p_ref[p, j, gs, :] = jnp.exp(s_ref[p, j, gs, :] - m).astype(bf16)

    def na_output(r, p_ref):
        w0, _ = window(r)
        q0 = pl.multiple_of(r * GRID_W, GRID_W)
        for p in range(NA_PAIRS):
            vw = jnp.concatenate([kv_ref[KV_VA + p, pl.ds(w0, win), :], ones], axis=1)
            pm = jnp.concatenate([p_ref[p, j] for j in range(n_blk)], axis=1)
            o2 = jnp.dot(pm, vw, preferred_element_type=f32)
            o2 = _slab(o2, 0) / _slab(o2, 1)
            o = jnp.where(low, o2[0:GRID_W], o2[GRID_W:])
            gate = tok_ref[TOK_GA + p, pl.ds(q0, GRID_W), :].astype(f32)
            mix_ref[p, pl.ds(q0, GRID_W), :] = (o * gate).astype(bf16)

    for t in range(n_rows + 2):
        if t < n_rows:
            na_scores(t, s_refs[t % 2], m_refs[t % 2])
        if 1 <= t <= n_rows:
            na_probs(s_refs[(t - 1) % 2], m_refs[(t - 1) % 2], p_refs[(t - 1) % 2])
        if t >= 2:
            na_output(t - 2, p_refs[t % 2])

    C = RET_CHUNK
    for c in range(tt // C):
        rws = slice(c * C, (c + 1) * C)
        for h in range(RET_HEADS):
            q = tok_ref[TOK_QR + h, rws, :]
            k = tok_ref[TOK_KR + h, rws, :]
            v = tok_ref[TOK_VR + h, rws, :]
            s = lax.dot_general(q, k, _NT, preferred_element_type=f32) * dmat_ref[h]
            o = jnp.dot(s.astype(bf16), v, preferred_element_type=f32)
            sf = state_ref[h]
            o = o + jnp.dot(q, sf.astype(bf16), preferred_element_type=f32) * dec_ref[DEC_QF, h]
            o = o + jnp.dot(q, sb_ref[c, h], preferred_element_type=f32) * dec_ref[DEC_QB, h]
            kd = (k.astype(f32) * dec_ref[DEC_KF, h]).astype(bf16)
            state_ref[h] = sf * cdec_ref[0, h, 0:1, :] + lax.dot_general(
                kd, v, _TN, preferred_element_type=f32)
            ms = jnp.mean(o * o, axis=-1, keepdims=True)
            rn = o * lax.rsqrt(ms + NORM_EPS) * og_ref[h]
            g = tok_ref[TOK_GR + h, rws, :].astype(f32)
            mix_ref[NA_PAIRS + h, rws, :] = (rn * g).astype(bf16)

    prev_ref[...] = mix_ref[...]


def _mix_out(x, mod3, tok_arr, kv, sb, bias, dmat, dec, cdec, og, w_out):
    B, T, _ = x.shape
    tt = MIX_TILE
    nt = T // tt
    C = RET_CHUNK
    rows = T // GRID_W
    win = NA_WIN_H * GRID_W
    assert T % tt == 0 and T >= tt + 2 * HALO_TOKENS, (T, tt)
    mixed = lambda i: jnp.minimum(i, nt - 1)
    projected = lambda b, i: (b, jnp.maximum(i - 1, 0), 0)
    c2 = lambda b, i: (0, 0)
    c3 = lambda b, i: (0, 0, 0)
    c4 = lambda b, i: (0, 0, 0, 0)
    halo_spec = pl.BlockSpec(
        (None, pl.Element(KV_SLABS), pl.Element(tt + 2 * HALO_TOKENS), pl.Element(LANES)),
        lambda b, i: (b, 0, _halo_start(mixed(i), tt, T), 0))
    score_shape = (NA_PAIRS, win // LANES, LANES, LANES)
    mix_shape = (NA_PAIRS + RET_HEADS, tt, LANES)
    return pl.pallas_call(
        functools.partial(_mix_out_kernel, rows=rows),
        grid=(B, nt + 1),
        in_specs=[pl.BlockSpec((None, tt, D_MODEL), projected),
                  pl.BlockSpec((None, 3, D_MODEL), lambda b, i: (b, 0, 0)),
                  pl.BlockSpec((None, TOK_SLABS, tt, LANES), lambda b, i: (b, 0, mixed(i), 0)),
                  halo_spec,
                  pl.BlockSpec((NA_PAIRS, BIAS_SLOTS, LANES, LANES), c4,
                               pipeline_mode=pl.Buffered(1)),
                  pl.BlockSpec((None, tt // C, RET_HEADS, C, C),
                               lambda b, i: (b, mixed(i), 0, 0, 0)),
                  pl.BlockSpec((RET_HEADS, C, C), c3),
                  pl.BlockSpec((DEC_TABLES, RET_HEADS, C, LANES), c4),
                  pl.BlockSpec((2, RET_HEADS, 8, LANES), c4),
                  pl.BlockSpec((RET_HEADS, 1, RET_HEAD_DIM), c3),
                  pl.BlockSpec((D_MODEL, D_MODEL), c2, pipeline_mode=pl.Buffered(1))],
        out_specs=pl.BlockSpec((None, tt, D_MODEL), projected),
        out_shape=jax.ShapeDtypeStruct((B, T, D_MODEL), jnp.float32),
        scratch_shapes=[pltpu.VMEM((RET_HEADS, C, C), jnp.float32),
                        pltpu.VMEM(mix_shape, jnp.bfloat16),
                        pltpu.VMEM(mix_shape, jnp.bfloat16),
                        pltpu.VMEM(score_shape, jnp.float32),
                        pltpu.VMEM(score_shape, jnp.float32),
                        pltpu.VMEM((NA_PAIRS, LANES, LANES), jnp.float32),
                        pltpu.VMEM((NA_PAIRS, LANES, LANES), jnp.float32),
                        pltpu.VMEM(score_shape, jnp.bfloat16),
                        pltpu.VMEM(score_shape, jnp.bfloat16)],
        compiler_params=pltpu.CompilerParams(dimension_semantics=("arbitrary", "arbitrary"),
                                             vmem_limit_bytes=VMEM_LIMIT_BYTES),
        name="mix_out",
    )(x, mod3, tok_arr, kv, bias, sb, dmat, dec, cdec, og, w_out)


def _rope_tables(positions):
    half = RET_HEAD_DIM // 2
    inv = ROPE_BASE ** (-jnp.arange(half, dtype=jnp.float32) / half)
    ang = positions.astype(jnp.float32)[:, None] * inv[None, :]
    cos, sin = jnp.cos(ang), jnp.sin(ang)
    return jnp.concatenate([cos, cos], axis=1), jnp.concatenate([sin, sin], axis=1)


def _layer(x, mod3, shared):
    (gain, w_in, bd, qg, kg, cosw, sinw, base, bias, dmat, dec, cdec, og, w_out) = shared
    nt = x.shape[1] // PROJ_TILE
    tok_arr, kv, sb = _in_proj(x, mod3, gain, w_in, bd, qg, kg, cosw, sinw, base[:nt], dec, cdec)
    return _mix_out(x, mod3, tok_arr, kv, sb, bias, dmat, dec, cdec, og, w_out)


def kernel(x_prompt, x_sample, c_prompt, c_sample, norm_gain, w_ada, b_ada, w_in, na_q_gain,
           na_k_gain, na_rpb, ret_decay_f, ret_decay_b, ret_out_gain, w_out):
    depth = norm_gain.shape[0]
    bp, bs = x_prompt.shape[0], x_sample.shape[0]
    nb = -(-(bp + bs) // 8) * 8
    t_max = max(x_prompt.shape[1], x_sample.shape[1])
    cosw, sinw = _rope_tables(jnp.arange(PROJ_TILE))
    base = jnp.stack(_rope_tables(jnp.arange(t_max // PROJ_TILE) * PROJ_TILE), axis=1)
    hid = np.arange(256) // NA_HEAD_DIM
    bd = jnp.asarray(hid[:, None] == hid[None, :], jnp.bfloat16)
    c_all = jnp.concatenate(
        [c_prompt, c_sample, jnp.zeros((nb - bp - bs, D_MODEL), jnp.float32)], axis=0)
    y_prompt, y_sample = x_prompt, x_sample
    for l in range(depth):
        rpb = jnp.pad(na_rpb[l], ((0, 0), (0, 0), (0, LANES - (2 * NA_WIN_W - 1))))
        mod, dmat, dec, cdec, bias = _prep(c_all, w_ada[l], b_ada[l][None],
                                           ret_decay_f[l], ret_decay_b[l], rpb)
        mod3 = mod.reshape(nb, 3, D_MODEL)
        w_seg = w_in[l].astype(jnp.bfloat16).reshape(D_MODEL, IN_WIDTH // SEG, SEG)
        shared = (norm_gain[l][None], w_seg.transpose(1, 0, 2), bd,
                  jnp.tile(na_q_gain[l], NA_HEADS)[None], jnp.tile(na_k_gain[l], NA_HEADS)[None],
                  cosw, sinw, base, bias, dmat, dec, cdec,
                  ret_out_gain[l].reshape(RET_HEADS, 1, RET_HEAD_DIM),
                  w_out[l].astype(jnp.bfloat16))
        y_prompt = _layer(y_prompt, mod3[:bp], shared)
        y_sample = _layer(y_sample, mod3[bp:bp + bs], shared)
    return (y_prompt, y_sample)
```

```python
import functools

import numpy as np
import jax
import jax.numpy as jnp
from jax import lax
from jax.experimental import pallas as pl
from jax.experimental.pallas import tpu as pltpu

D_MODEL = 1024
GRID_W = 64
NA_HEADS = 8
NA_HEAD_DIM = 64
NA_WIDTH = NA_HEADS * NA_HEAD_DIM
NA_PAIRS = NA_HEADS // 2
NA_WIN_H = 8
NA_WIN_W = 16
RET_HEADS = 4
RET_HEAD_DIM = 128
RET_WIDTH = RET_HEADS * RET_HEAD_DIM
RET_CHUNK = 128
ROPE_BASE = 10000.0
NORM_EPS = 1e-6
IN_WIDTH = 4 * NA_WIDTH + 4 * RET_WIDTH
SEG = 512
LANES = 128

PROJ_TILE = 1024
MIX_TILE = 1024
HALO_ROWS = NA_WIN_H // 2
HALO_TOKENS = HALO_ROWS * GRID_W
BIAS_SLOTS = 2 * NA_WIN_H - 2
NA_GROUP = 16

TOK_QA, TOK_GA, TOK_QR, TOK_KR, TOK_VR, TOK_GR = 0, 4, 8, 12, 16, 20
TOK_SLABS = 24
KV_KA, KV_VA = 0, NA_PAIRS
KV_SLABS = 2 * NA_PAIRS
DEC_QF, DEC_QB, DEC_KF, DEC_KB = range(4)
DEC_TABLES = 4
VMEM_LIMIT_BYTES = 56 * 1024 * 1024

_NT = (((1,), (1,)), ((), ()))
_TN = (((0,), (0,)), ((), ()))


def _silu(v):
    return v / (1.0 + jnp.exp(-v))


def _slab(v, n):
    return v[:, n * LANES:(n + 1) * LANES]


def _prep_kernel(dec_f_ref, dec_b_ref, c_ref, w_ref, b_ref, rpb_ref,
                 mod_ref, dmat_ref, dec_ref, cdec_ref, bias_ref):
    c = c_ref[...]
    mod_ref[...] = jnp.dot(_silu(c), w_ref[...], preferred_element_type=jnp.float32) + b_ref[...]

    @pl.when(pl.program_id(0) == 0)
    def _():
        C = RET_CHUNK
        pos = lax.broadcasted_iota(jnp.int32, (C, LANES), 0).astype(jnp.float32)
        ri = lax.broadcasted_iota(jnp.int32, (C, C), 0)
        ci = lax.broadcasted_iota(jnp.int32, (C, C), 1)
        diff = (ri - ci).astype(jnp.float32)
        for h in range(RET_HEADS):
            lgf = -jnp.exp(jnp.full((1, LANES), dec_f_ref[h], jnp.float32))
            lgb = -jnp.exp(jnp.full((1, LANES), dec_b_ref[h], jnp.float32))
            dec_ref[DEC_QF, h] = jnp.exp(lgf * (pos + 1.0))
            dec_ref[DEC_QB, h] = jnp.exp(lgb * (C - pos))
            dec_ref[DEC_KF, h] = jnp.exp(lgf * (C - 1.0 - pos))
            dec_ref[DEC_KB, h] = jnp.exp(lgb * pos)
            cdec_ref[0, h] = jnp.broadcast_to(jnp.exp(lgf * C), (8, LANES))
            cdec_ref[1, h] = jnp.broadcast_to(jnp.exp(lgb * C), (8, LANES))
            dmat_ref[h] = jnp.where(diff >= 0, jnp.exp(lgf * jnp.maximum(diff, 0.0)),
                                    jnp.exp(lgb * jnp.maximum(-diff, 0.0)))

        qcol = lax.broadcasted_iota(jnp.int32, (GRID_W, LANES), 0)
        lane = lax.broadcasted_iota(jnp.int32, (GRID_W, LANES), 1)
        kcol = lane % GRID_W
        wstart = jnp.clip(qcol - NA_WIN_W // 2, 0, GRID_W - NA_WIN_W)
        valid = (kcol >= wstart) & (kcol < wstart + NA_WIN_W)
        centre = LANES - (NA_WIN_W - 1)
        for h in range(NA_HEADS):
            for d in range(BIAS_SLOTS):
                even = pltpu.roll(jnp.broadcast_to(rpb_ref[h, d:d + 1, :], (GRID_W, LANES)),
                                  centre, 1, stride=1, stride_axis=0)
                odd = pltpu.roll(jnp.broadcast_to(rpb_ref[h, d + 1:d + 2, :], (GRID_W, LANES)),
                                 (centre + GRID_W) % LANES, 1, stride=1, stride_axis=0)
                blk = jnp.where(valid, jnp.where(lane < GRID_W, even, odd), -jnp.inf)
                bias_ref[h // 2, d, (h % 2) * GRID_W:(h % 2 + 1) * GRID_W, :] = blk


def _prep(c_all, w_ada, b_ada, dec_f, dec_b, rpb):
    nb = c_all.shape[0]
    C = RET_CHUNK
    f32 = jnp.float32
    smem = pl.BlockSpec(memory_space=pltpu.SMEM)
    const2 = lambda j: (0, 0)
    const3 = lambda j: (0, 0, 0)
    const4 = lambda j: (0, 0, 0, 0)
    return pl.pallas_call(
        _prep_kernel,
        grid=(3,),
        in_specs=[smem, smem,
                  pl.BlockSpec((nb, D_MODEL), const2),
                  pl.BlockSpec((D_MODEL, D_MODEL), lambda j: (0, j)),
                  pl.BlockSpec((1, D_MODEL), lambda j: (0, j)),
                  pl.BlockSpec((NA_HEADS, 2 * NA_WIN_H - 1, LANES), const3)],
        out_specs=[pl.BlockSpec((nb, D_MODEL), lambda j: (0, j)),
                   pl.BlockSpec((RET_HEADS, C, C), const3),
                   pl.BlockSpec((DEC_TABLES, RET_HEADS, C, LANES), const4),
                   pl.BlockSpec((2, RET_HEADS, 8, LANES), const4),
                   pl.BlockSpec((NA_PAIRS, BIAS_SLOTS, LANES, LANES), const4)],
        out_shape=[jax.ShapeDtypeStruct((nb, 3 * D_MODEL), f32),
                   jax.ShapeDtypeStruct((RET_HEADS, C, C), f32),
                   jax.ShapeDtypeStruct((DEC_TABLES, RET_HEADS, C, LANES), f32),
                   jax.ShapeDtypeStruct((2, RET_HEADS, 8, LANES), f32),
                   jax.ShapeDtypeStruct((NA_PAIRS, BIAS_SLOTS, LANES, LANES), f32)],
        compiler_params=pltpu.CompilerParams(dimension_semantics=("arbitrary",),
                                             vmem_limit_bytes=VMEM_LIMIT_BYTES),
        name="prep",
    )(dec_f, dec_b, c_all, w_ada, b_ada, rpb)


def _in_proj_kernel(x_ref, mod_ref, gain_ref, w_ref, bd_ref, qg_ref, kg_ref, cos_ref, sin_ref,
                    base_ref, dec_ref, cdec_ref,
                    tok_ref, kv_ref, sb_ref,
                    state_ref):
    bf16, f32 = jnp.bfloat16, jnp.float32
    tt = x_ref.shape[0]

    @pl.when(pl.program_id(1) == 0)
    def _():
        state_ref[...] = jnp.zeros_like(state_ref)

    x = x_ref[...]
    ms = jnp.mean(x * x, axis=-1, keepdims=True)
    a = gain_ref[...] * (1.0 + mod_ref[1:2, :])
    hb = (x * lax.rsqrt(ms + NORM_EPS) * a + mod_ref[0:1, :]).astype(bf16)

    def seg(s):
        return jnp.dot(hb, w_ref[s], preferred_element_type=f32)

    def head_norm(p, g):
        pp = (p * p).astype(bf16)
        ss = jnp.concatenate(
            [jnp.dot(pp[:, c * 256:(c + 1) * 256], bd_ref[...], preferred_element_type=f32)
             for c in range(NA_WIDTH // 256)], axis=1)
        return p * lax.rsqrt(ss * (1.0 / NA_HEAD_DIM) + NORM_EPS) * g

    def put(ref, first, val):
        for n in range(val.shape[1] // LANES):
            ref[first + n] = _slab(val, n)

    put(tok_ref, TOK_QA, head_norm(seg(0), qg_ref[...] * (NA_HEAD_DIM ** -0.5)).astype(bf16))
    put(kv_ref, KV_KA, head_norm(seg(1), kg_ref[...]).astype(bf16))
    put(kv_ref, KV_VA, seg(2).astype(bf16))
    put(tok_ref, TOK_GA, _silu(seg(3)).astype(bf16))

    c0, s0 = base_ref[0:1, :], base_ref[1:2, :]
    cw, sw = cos_ref[...], sin_ref[...]
    hlane = lax.broadcasted_iota(jnp.int32, (1, RET_HEAD_DIM), 1)
    cosf = c0 * cw - s0 * sw
    sinf = (s0 * cw + c0 * sw) * jnp.where(hlane < RET_HEAD_DIM // 2, -1.0, 1.0)

    def rotary(p):
        outs = []
        for h in range(RET_HEADS):
            ph = _slab(p, h)
            outs.append(ph * cosf + pltpu.roll(ph, RET_HEAD_DIM // 2, 1) * sinf)
        return jnp.concatenate(outs, axis=1)

    put(tok_ref, TOK_QR, rotary(seg(4)).astype(bf16))
    kr = rotary(seg(5)) * (RET_HEAD_DIM ** -0.5)
    put(tok_ref, TOK_KR, kr.astype(bf16))
    vb = seg(6).astype(bf16)
    put(tok_ref, TOK_VR, vb)
    put(tok_ref, TOK_GR, _silu(seg(7)).astype(bf16))

    C = RET_CHUNK
    for c in reversed(range(tt // C)):
        rows = slice(c * C, (c + 1) * C)
        for h in range(RET_HEADS):
            s_old = state_ref[h]
            sb_ref[c, h] = s_old.astype(bf16)
            kd = (_slab(kr[rows], h) * dec_ref[DEC_KB, h]).astype(bf16)
            kv = lax.dot_general(kd, _slab(vb[rows], h), _TN, preferred_element_type=f32)
            state_ref[h] = s_old * cdec_ref[1, h, 0:1, :] + kv


def _in_proj(x, mod3, gain, w_in, bd, qg, kg, cosw, sinw, base, dec, cdec):
    B, T, _ = x.shape
    tt = PROJ_TILE
    nt = T // tt
    C = RET_CHUNK
    rev = lambda b, i: (b, nt - 1 - i, 0)
    rev4 = lambda b, i: (b, 0, nt - 1 - i, 0)
    c2 = lambda b, i: (0, 0)
    c3 = lambda b, i: (0, 0, 0)
    c4 = lambda b, i: (0, 0, 0, 0)
    return pl.pallas_call(
        _in_proj_kernel,
        grid=(B, nt),
        in_specs=[pl.BlockSpec((None, tt, D_MODEL), rev),
                  pl.BlockSpec((None, 3, D_MODEL), lambda b, i: (b, 0, 0)),
                  pl.BlockSpec((1, D_MODEL), c2),
                  pl.BlockSpec((IN_WIDTH // SEG, D_MODEL, SEG), c3, pipeline_mode=pl.Buffered(1)),
                  pl.BlockSpec((256, 256), c2),
                  pl.BlockSpec((1, NA_WIDTH), c2),
                  pl.BlockSpec((1, NA_WIDTH), c2),
                  pl.BlockSpec((tt, RET_HEAD_DIM), c2),
                  pl.BlockSpec((tt, RET_HEAD_DIM), c2),
                  pl.BlockSpec((None, 2, RET_HEAD_DIM), lambda b, i: (nt - 1 - i, 0, 0)),
                  pl.BlockSpec((DEC_TABLES, RET_HEADS, C, LANES), c4),
                  pl.BlockSpec((2, RET_HEADS, 8, LANES), c4)],
        out_specs=[pl.BlockSpec((None, TOK_SLABS, tt, LANES), rev4),
                   pl.BlockSpec((None, KV_SLABS, tt, LANES), rev4),
                   pl.BlockSpec((None, tt // C, RET_HEADS, C, C),
                                lambda b, i: (b, nt - 1 - i, 0, 0, 0))],
        out_shape=[jax.ShapeDtypeStruct((B, TOK_SLABS, T, LANES), jnp.bfloat16),
                   jax.ShapeDtypeStruct((B, KV_SLABS, T, LANES), jnp.bfloat16),
                   jax.ShapeDtypeStruct((B, T // C, RET_HEADS, C, C), jnp.bfloat16)],
        scratch_shapes=[pltpu.VMEM((RET_HEADS, C, C), jnp.float32)],
        compiler_params=pltpu.CompilerParams(dimension_semantics=("arbitrary", "arbitrary"),
                                             vmem_limit_bytes=VMEM_LIMIT_BYTES),
        name="in_proj",
    )(x, mod3, gain, w_in, bd, qg, kg, cosw, sinw, base, dec, cdec)


def _halo_start(i, tt, seq_len):
    units = jnp.clip(i * (tt // HALO_TOKENS) - 1, 0, (seq_len - tt) // HALO_TOKENS - 2)
    return units * HALO_TOKENS


def _mix_out_kernel(x_ref, mod_ref, tok_ref, kv_ref, bias_ref, sb_ref,
                    dmat_ref, dec_ref, cdec_ref, og_ref, wo_ref,
                    y_ref,
                    state_ref, mix_ref, s0_ref, s1_ref, m0_ref, m1_ref,
                    p0_ref, p1_ref, *, rows):
    bf16, f32 = jnp.bfloat16, jnp.float32
    tt = x_ref.shape[0]
    n_rows = tt // GRID_W
    i = pl.program_id(1)

    @pl.when(i == 0)
    def _():
        state_ref[...] = jnp.zeros_like(state_ref)

    first_row = _halo_start(i, tt, rows * GRID_W) // GRID_W

    lane = lax.broadcasted_iota(jnp.int32, (GRID_W, LANES), 1)
    low = lane < NA_HEAD_DIM
    win = NA_WIN_H * GRID_W
    n_blk = win // LANES
    ones = jnp.ones((win, LANES), bf16)

    def window(r):
        grow = i * n_rows + r
        start = jnp.clip(grow - NA_WIN_H // 2, 0, rows - NA_WIN_H)
        w0 = pl.multiple_of((start - first_row) * GRID_W, GRID_W)
        return w0, grow - start

    def na_scores(r, s_ref, m_ref):
        w0, off = window(r)
        q0 = pl.multiple_of(r * GRID_W, GRID_W)
        for p in range(NA_PAIRS):
            q = tok_ref[TOK_QA + p, pl.ds(q0, GRID_W), :]
            q2 = jnp.concatenate([jnp.where(low, q, jnp.zeros_like(q)),
                                  jnp.where(low, jnp.zeros_like(q), q)], axis=0)
            kw = kv_ref[KV_KA + p, pl.ds(w0, win), :]
            s = lax.dot_general(q2, kw, _NT, preferred_element_type=f32)
            for g in range(LANES // NA_GROUP):
                gs = slice(g * NA_GROUP, (g + 1) * NA_GROUP)
                m = None
                for j in range(n_blk):
                    blk = _slab(s[gs], j) + bias_ref[p, 2 * j - off + (NA_WIN_H - 1), gs, :]
                    s_ref[p, j, gs, :] = blk
                    m = blk if m is None else jnp.maximum(m, blk)
                m_ref[p, gs, :] = jnp.broadcast_to(jnp.max(m, axis=-1, keepdims=True),
                                                   (NA_GROUP, LANES))

    def na_probs(s_ref, m_ref, p_ref):
        for p in range(NA_PAIRS):
            for g in range(LANES // NA_GROUP):
                gs = slice(g * NA_GROUP, (g + 1) * NA_GROUP)
                m = m_ref[p, gs, :]
                for j in range(n_blk):
                    p_ref[p, j, gs, :] = jnp.exp(s_ref[p, j, gs, :] - m).astype(bf16)

    def na_output(r, p_ref):
        w0, _ = window(r)
        q0 = pl.multiple_of(r * GRID_W, GRID_W)
        for p in range(NA_PAIRS):
            vw = jnp.concatenate([kv_ref[KV_VA + p, pl.ds(w0, win), :], ones], axis=1)
            pm = jnp.concatenate([p_ref[p, j] for j in range(n_blk)], axis=1)
            o2 = jnp.dot(pm, vw, preferred_element_type=f32)
            o2 = _slab(o2, 0) / _slab(o2, 1)
            o = jnp.where(low, o2[0:GRID_W], o2[GRID_W:])
            gate = tok_ref[TOK_GA + p, pl.ds(q0, GRID_W), :].astype(f32)
            mix_ref[p, pl.ds(q0, GRID_W), :] = (o * gate).astype(bf16)

    s_refs, m_refs, p_refs = (s0_ref, s1_ref), (m0_ref, m1_ref), (p0_ref, p1_ref)
    for t in range(n_rows + 2):
        if t < n_rows:
            na_scores(t, s_refs[t % 2], m_refs[t % 2])
        if 1 <= t <= n_rows:
            na_probs(s_refs[(t - 1) % 2], m_refs[(t - 1) % 2], p_refs[(t - 1) % 2])
        if t >= 2:
            na_output(t - 2, p_refs[t % 2])

    C = RET_CHUNK
    for c in range(tt // C):
        rws = slice(c * C, (c + 1) * C)
        for h in range(RET_HEADS):
            q = tok_ref[TOK_QR + h, rws, :]
            k = tok_ref[TOK_KR + h, rws, :]
            v = tok_ref[TOK_VR + h, rws, :]
            s = lax.dot_general(q, k, _NT, preferred_element_type=f32) * dmat_ref[h]
            o = jnp.dot(s.astype(bf16), v, preferred_element_type=f32)
            sf = state_ref[h]
            o = o + jnp.dot(q, sf.astype(bf16), preferred_element_type=f32) * dec_ref[DEC_QF, h]
            o = o + jnp.dot(q, sb_ref[c, h], preferred_element_type=f32) * dec_ref[DEC_QB, h]
            kd = (k.astype(f32) * dec_ref[DEC_KF, h]).astype(bf16)
            state_ref[h] = sf * cdec_ref[0, h, 0:1, :] + lax.dot_general(
                kd, v, _TN, preferred_element_type=f32)
            ms = jnp.mean(o * o, axis=-1, keepdims=True)
            rn = o * lax.rsqrt(ms + NORM_EPS) * og_ref[h]
            g = tok_ref[TOK_GR + h, rws, :].astype(f32)
            mix_ref[NA_PAIRS + h, rws, :] = (rn * g).astype(bf16)

    mix = jnp.concatenate([mix_ref[n] for n in range(NA_PAIRS + RET_HEADS)], axis=1)
    out = jnp.dot(mix, wo_ref[...], preferred_element_type=f32)
    y_ref[...] = x_ref[...] + mod_ref[2:3, :] * out


def _mix_out(x, mod3, tok_arr, kv, sb, bias, dmat, dec, cdec, og, w_out):
    B, T, _ = x.shape
    tt = MIX_TILE
    nt = T // tt
    C = RET_CHUNK
    rows = T // GRID_W
    win = NA_WIN_H * GRID_W
    assert T % tt == 0 and T >= tt + 2 * HALO_TOKENS, (T, tt)
    tok = lambda b, i: (b, i, 0)
    c2 = lambda b, i: (0, 0)
    c3 = lambda b, i: (0, 0, 0)
    c4 = lambda b, i: (0, 0, 0, 0)
    halo_spec = pl.BlockSpec(
        (None, pl.Element(KV_SLABS), pl.Element(tt + 2 * HALO_TOKENS), pl.Element(LANES)),
        lambda b, i: (b, 0, _halo_start(i, tt, T), 0))
    score_shape = (NA_PAIRS, win // LANES, LANES, LANES)
    return pl.pallas_call(
        functools.partial(_mix_out_kernel, rows=rows),
        grid=(B, nt),
        in_specs=[pl.BlockSpec((None, tt, D_MODEL), tok),
                  pl.BlockSpec((None, 3, D_MODEL), lambda b, i: (b, 0, 0)),
                  pl.BlockSpec((None, TOK_SLABS, tt, LANES), lambda b, i: (b, 0, i, 0)),
                  halo_spec,
                  pl.BlockSpec((NA_PAIRS, BIAS_SLOTS, LANES, LANES), c4,
                               pipeline_mode=pl.Buffered(1)),
                  pl.BlockSpec((None, tt // C, RET_HEADS, C, C), lambda b, i: (b, i, 0, 0, 0)),
                  pl.BlockSpec((RET_HEADS, C, C), c3),
                  pl.BlockSpec((DEC_TABLES, RET_HEADS, C, LANES), c4),
                  pl.BlockSpec((2, RET_HEADS, 8, LANES), c4),
                  pl.BlockSpec((RET_HEADS, 1, RET_HEAD_DIM), c3),
                  pl.BlockSpec((D_MODEL, D_MODEL), c2, pipeline_mode=pl.Buffered(1))],
        out_specs=pl.BlockSpec((None, tt, D_MODEL), tok),
        out_shape=jax.ShapeDtypeStruct((B, T, D_MODEL), jnp.float32),
        scratch_shapes=[pltpu.VMEM((RET_HEADS, C, C), jnp.float32),
                        pltpu.VMEM((NA_PAIRS + RET_HEADS, tt, LANES), jnp.bfloat16),
                        pltpu.VMEM(score_shape, jnp.float32),
                        pltpu.VMEM(score_shape, jnp.float32),
                        pltpu.VMEM((NA_PAIRS, LANES, LANES), jnp.float32),
                        pltpu.VMEM((NA_PAIRS, LANES, LANES), jnp.float32),
                        pltpu.VMEM(score_shape, jnp.bfloat16),
                        pltpu.VMEM(score_shape, jnp.bfloat16)],
        compiler_params=pltpu.CompilerParams(dimension_semantics=("arbitrary", "arbitrary"),
                                             vmem_limit_bytes=VMEM_LIMIT_BYTES),
        name="mix_out",
    )(x, mod3, tok_arr, kv, bias, sb, dmat, dec, cdec, og, w_out)


def _rope_tables(positions):
    half = RET_HEAD_DIM // 2
    inv = ROPE_BASE ** (-jnp.arange(half, dtype=jnp.float32) / half)
    ang = positions.astype(jnp.float32)[:, None] * inv[None, :]
    cos, sin = jnp.cos(ang), jnp.sin(ang)
    return jnp.concatenate([cos, cos], axis=1), jnp.concatenate([sin, sin], axis=1)


def _layer(x, mod3, shared):
    (gain, w_in, bd, qg, kg, cosw, sinw, base, bias, dmat, dec, cdec, og, w_out) = shared
    nt = x.shape[1] // PROJ_TILE
    tok_arr, kv, sb = _in_proj(x, mod3, gain, w_in, bd, qg, kg, cosw, sinw, base[:nt], dec, cdec)
    return _mix_out(x, mod3, tok_arr, kv, sb, bias, dmat, dec, cdec, og, w_out)


def kernel(x_prompt, x_sample, c_prompt, c_sample, norm_gain, w_ada, b_ada, w_in, na_q_gain,
           na_k_gain, na_rpb, ret_decay_f, ret_decay_b, ret_out_gain, w_out):
    depth = norm_gain.shape[0]
    bp, bs = x_prompt.shape[0], x_sample.shape[0]
    nb = -(-(bp + bs) // 8) * 8
    t_max = max(x_prompt.shape[1], x_sample.shape[1])
    cosw, sinw = _rope_tables(jnp.arange(PROJ_TILE))
    base = jnp.stack(_rope_tables(jnp.arange(t_max // PROJ_TILE) * PROJ_TILE), axis=1)
    hid = np.arange(256) // NA_HEAD_DIM
    bd = jnp.asarray(hid[:, None] == hid[None, :], jnp.bfloat16)
    c_all = jnp.concatenate(
        [c_prompt, c_sample, jnp.zeros((nb - bp - bs, D_MODEL), jnp.float32)], axis=0)
    y_prompt, y_sample = x_prompt, x_sample
    for l in range(depth):
        rpb = jnp.pad(na_rpb[l], ((0, 0), (0, 0), (0, LANES - (2 * NA_WIN_W - 1))))
        mod, dmat, dec, cdec, bias = _prep(c_all, w_ada[l], b_ada[l][None],
                                           ret_decay_f[l], ret_decay_b[l], rpb)
        mod3 = mod.reshape(nb, 3, D_MODEL)
        w_seg = w_in[l].astype(jnp.bfloat16).reshape(D_MODEL, IN_WIDTH // SEG, SEG)
        shared = (norm_gain[l][None], w_seg.transpose(1, 0, 2), bd,
                  jnp.tile(na_q_gain[l], NA_HEADS)[None], jnp.tile(na_k_gain[l], NA_HEADS)[None],
                  cosw, sinw, base, bias, dmat, dec, cdec,
                  ret_out_gain[l].reshape(RET_HEADS, 1, RET_HEAD_DIM),
                  w_out[l].astype(jnp.bfloat16))
        y_prompt = _layer(y_prompt, mod3[:bp], shared)
        y_sample = _layer(y_sample, mod3[bp:bp + bs], shared)
    return (y_prompt, y_sample)
```

```python
import functools

import numpy as np
import jax
import jax.numpy as jnp
from jax import lax
from jax.experimental import pallas as pl
from jax.experimental.pallas import tpu as pltpu

D_MODEL = 1024
GRID_W = 64
NA_HEADS = 8
NA_HEAD_DIM = 64
NA_WIDTH = NA_HEADS * NA_HEAD_DIM
NA_PAIRS = NA_HEADS // 2
NA_WIN_H = 8
NA_WIN_W = 16
RET_HEADS = 4
RET_HEAD_DIM = 128
RET_WIDTH = RET_HEADS * RET_HEAD_DIM
RET_CHUNK = 128
ROPE_BASE = 10000.0
NORM_EPS = 1e-6
LOG2E = 1.4426950408889634
IN_WIDTH = 4 * NA_WIDTH + 4 * RET_WIDTH
SEG = 512
LANES = 128

PROJ_TILE = 1024
MIX_TILE = 1024
HALO_ROWS = NA_WIN_H // 2
HALO_TOKENS = HALO_ROWS * GRID_W
BIAS_SLOTS = 2 * NA_WIN_H - 2
NA_GROUP = 16

TOK_QA, TOK_GA, TOK_QR, TOK_KR, TOK_VR, TOK_GR = 0, 4, 8, 12, 16, 20
TOK_SLABS = 24
KV_KA, KV_VA = 0, NA_PAIRS
KV_SLABS = 2 * NA_PAIRS
DEC_QF, DEC_QB, DEC_KF, DEC_KB = range(4)
DEC_TABLES = 4
VMEM_LIMIT_BYTES = 56 * 1024 * 1024

_NT = (((1,), (1,)), ((), ()))
_TN = (((0,), (0,)), ((), ()))


def _silu(v):
    return v / (1.0 + jnp.exp(-v))


def _slab(v, n):
    return v[:, n * LANES:(n + 1) * LANES]


def _prep_kernel(dec_f_ref, dec_b_ref, c_ref, w_ref, b_ref, rpb_ref,
                 mod_ref, dmat_ref, dec_ref, cdec_ref, bias_ref):
    c = c_ref[...]
    mod_ref[...] = jnp.dot(_silu(c), w_ref[...], preferred_element_type=jnp.float32) + b_ref[...]

    @pl.when(pl.program_id(0) == 0)
    def _():
        C = RET_CHUNK
        pos = lax.broadcasted_iota(jnp.int32, (C, LANES), 0).astype(jnp.float32)
        ri = lax.broadcasted_iota(jnp.int32, (C, C), 0)
        ci = lax.broadcasted_iota(jnp.int32, (C, C), 1)
        diff = (ri - ci).astype(jnp.float32)
        for h in range(RET_HEADS):
            lgf = -jnp.exp(jnp.full((1, LANES), dec_f_ref[h], jnp.float32))
            lgb = -jnp.exp(jnp.full((1, LANES), dec_b_ref[h], jnp.float32))
            dec_ref[DEC_QF, h] = jnp.exp(lgf * (pos + 1.0))
            dec_ref[DEC_QB, h] = jnp.exp(lgb * (C - pos))
            dec_ref[DEC_KF, h] = jnp.exp(lgf * (C - 1.0 - pos))
            dec_ref[DEC_KB, h] = jnp.exp(lgb * pos)
            cdec_ref[0, h] = jnp.broadcast_to(jnp.exp(lgf * C), (8, LANES))
            cdec_ref[1, h] = jnp.broadcast_to(jnp.exp(lgb * C), (8, LANES))
            dmat_ref[h] = jnp.where(diff >= 0, jnp.exp(lgf * jnp.maximum(diff, 0.0)),
                                    jnp.exp(lgb * jnp.maximum(-diff, 0.0)))

        qcol = lax.broadcasted_iota(jnp.int32, (GRID_W, LANES), 0)
        lane = lax.broadcasted_iota(jnp.int32, (GRID_W, LANES), 1)
        kcol = lane % GRID_W
        wstart = jnp.clip(qcol - NA_WIN_W // 2, 0, GRID_W - NA_WIN_W)
        valid = (kcol >= wstart) & (kcol < wstart + NA_WIN_W)
        centre = LANES - (NA_WIN_W - 1)
        for h in range(NA_HEADS):
            for d in range(BIAS_SLOTS):
                even = pltpu.roll(jnp.broadcast_to(rpb_ref[h, d:d + 1, :], (GRID_W, LANES)),
                                  centre, 1, stride=1, stride_axis=0)
                odd = pltpu.roll(jnp.broadcast_to(rpb_ref[h, d + 1:d + 2, :], (GRID_W, LANES)),
                                 (centre + GRID_W) % LANES, 1, stride=1, stride_axis=0)
                blk = jnp.where(valid, jnp.where(lane < GRID_W, even, odd) * LOG2E, -jnp.inf)
                bias_ref[h // 2, d, (h % 2) * GRID_W:(h % 2 + 1) * GRID_W, :] = blk


def _prep(c_all, w_ada, b_ada, dec_f, dec_b, rpb):
    nb = c_all.shape[0]
    C = RET_CHUNK
    f32 = jnp.float32
    smem = pl.BlockSpec(memory_space=pltpu.SMEM)
    const2 = lambda j: (0, 0)
    const3 = lambda j: (0, 0, 0)
    const4 = lambda j: (0, 0, 0, 0)
    return pl.pallas_call(
        _prep_kernel,
        grid=(3,),
        in_specs=[smem, smem,
                  pl.BlockSpec((nb, D_MODEL), const2),
                  pl.BlockSpec((D_MODEL, D_MODEL), lambda j: (0, j)),
                  pl.BlockSpec((1, D_MODEL), lambda j: (0, j)),
                  pl.BlockSpec((NA_HEADS, 2 * NA_WIN_H - 1, LANES), const3)],
        out_specs=[pl.BlockSpec((nb, D_MODEL), lambda j: (0, j)),
                   pl.BlockSpec((RET_HEADS, C, C), const3),
                   pl.BlockSpec((DEC_TABLES, RET_HEADS, C, LANES), const4),
                   pl.BlockSpec((2, RET_HEADS, 8, LANES), const4),
                   pl.BlockSpec((NA_PAIRS, BIAS_SLOTS, LANES, LANES), const4)],
        out_shape=[jax.ShapeDtypeStruct((nb, 3 * D_MODEL), f32),
                   jax.ShapeDtypeStruct((RET_HEADS, C, C), f32),
                   jax.ShapeDtypeStruct((DEC_TABLES, RET_HEADS, C, LANES), f32),
                   jax.ShapeDtypeStruct((2, RET_HEADS, 8, LANES), f32),
                   jax.ShapeDtypeStruct((NA_PAIRS, BIAS_SLOTS, LANES, LANES), f32)],
        compiler_params=pltpu.CompilerParams(dimension_semantics=("arbitrary",),
                                             vmem_limit_bytes=VMEM_LIMIT_BYTES),
        name="prep",
    )(dec_f, dec_b, c_all, w_ada, b_ada, rpb)


def _in_proj_kernel(x_ref, mod_ref, gain_ref, w_ref, bd_ref, qg_ref, kg_ref, cos_ref, sin_ref,
                    base_ref, dec_ref, cdec_ref,
                    tok_ref, kv_ref, sb_ref,
                    state_ref):
    bf16, f32 = jnp.bfloat16, jnp.float32
    tt = x_ref.shape[0]

    @pl.when(pl.program_id(1) == 0)
    def _():
        state_ref[...] = jnp.zeros_like(state_ref)

    x = x_ref[...]
    ms = jnp.mean(x * x, axis=-1, keepdims=True)
    a = gain_ref[...] * (1.0 + mod_ref[1:2, :])
    hb = (x * lax.rsqrt(ms + NORM_EPS) * a + mod_ref[0:1, :]).astype(bf16)

    def seg(s):
        return jnp.dot(hb, w_ref[s], preferred_element_type=f32)

    def head_norm(p, g):
        pp = (p * p).astype(bf16)
        ss = jnp.concatenate(
            [jnp.dot(pp[:, c * 256:(c + 1) * 256], bd_ref[...], preferred_element_type=f32)
             for c in range(NA_WIDTH // 256)], axis=1)
        return p * lax.rsqrt(ss * (1.0 / NA_HEAD_DIM) + NORM_EPS) * g

    def put(ref, first, val):
        for n in range(val.shape[1] // LANES):
            ref[first + n] = _slab(val, n)

    put(tok_ref, TOK_QA,
        head_norm(seg(0), qg_ref[...] * (NA_HEAD_DIM ** -0.5 * LOG2E)).astype(bf16))
    put(kv_ref, KV_KA, head_norm(seg(1), kg_ref[...]).astype(bf16))
    put(kv_ref, KV_VA, seg(2).astype(bf16))
    put(tok_ref, TOK_GA, _silu(seg(3)).astype(bf16))

    c0, s0 = base_ref[0:1, :], base_ref[1:2, :]
    cw, sw = cos_ref[...], sin_ref[...]
    hlane = lax.broadcasted_iota(jnp.int32, (1, RET_HEAD_DIM), 1)
    cosf = c0 * cw - s0 * sw
    sinf = (s0 * cw + c0 * sw) * jnp.where(hlane < RET_HEAD_DIM // 2, -1.0, 1.0)

    def rotary(p):
        outs = []
        for h in range(RET_HEADS):
            ph = _slab(p, h)
            outs.append(ph * cosf + pltpu.roll(ph, RET_HEAD_DIM // 2, 1) * sinf)
        return jnp.concatenate(outs, axis=1)

    put(tok_ref, TOK_QR, rotary(seg(4)).astype(bf16))
    kr = rotary(seg(5)) * (RET_HEAD_DIM ** -0.5)
    put(tok_ref, TOK_KR, kr.astype(bf16))
    vb = seg(6).astype(bf16)
    put(tok_ref, TOK_VR, vb)
    put(tok_ref, TOK_GR, _silu(seg(7)).astype(bf16))

    C = RET_CHUNK
    for c in reversed(range(tt // C)):
        rows = slice(c * C, (c + 1) * C)
        for h in range(RET_HEADS):
            s_old = state_ref[h]
            sb_ref[c, h] = s_old.astype(bf16)
            kd = (_slab(kr[rows], h) * dec_ref[DEC_KB, h]).astype(bf16)
            kv = lax.dot_general(kd, _slab(vb[rows], h), _TN, preferred_element_type=f32)
            state_ref[h] = s_old * cdec_ref[1, h, 0:1, :] + kv


def _in_proj(x, mod3, gain, w_in, bd, qg, kg, cosw, sinw, base, dec, cdec):
    B, T, _ = x.shape
    tt = PROJ_TILE
    nt = T // tt
    C = RET_CHUNK
    rev = lambda b, i: (b, nt - 1 - i, 0)
    rev4 = lambda b, i: (b, 0, nt - 1 - i, 0)
    c2 = lambda b, i: (0, 0)
    c3 = lambda b, i: (0, 0, 0)
    c4 = lambda b, i: (0, 0, 0, 0)
    return pl.pallas_call(
        _in_proj_kernel,
        grid=(B, nt),
        in_specs=[pl.BlockSpec((None, tt, D_MODEL), rev),
                  pl.BlockSpec((None, 3, D_MODEL), lambda b, i: (b, 0, 0)),
                  pl.BlockSpec((1, D_MODEL), c2),
                  pl.BlockSpec((IN_WIDTH // SEG, D_MODEL, SEG), c3, pipeline_mode=pl.Buffered(1)),
                  pl.BlockSpec((256, 256), c2),
                  pl.BlockSpec((1, NA_WIDTH), c2),
                  pl.BlockSpec((1, NA_WIDTH), c2),
                  pl.BlockSpec((tt, RET_HEAD_DIM), c2),
                  pl.BlockSpec((tt, RET_HEAD_DIM), c2),
                  pl.BlockSpec((None, 2, RET_HEAD_DIM), lambda b, i: (nt - 1 - i, 0, 0)),
                  pl.BlockSpec((DEC_TABLES, RET_HEADS, C, LANES), c4),
                  pl.BlockSpec((2, RET_HEADS, 8, LANES), c4)],
        out_specs=[pl.BlockSpec((None, TOK_SLABS, tt, LANES), rev4),
                   pl.BlockSpec((None, KV_SLABS, tt, LANES), rev4),
                   pl.BlockSpec((None, tt // C, RET_HEADS, C, C),
                                lambda b, i: (b, nt - 1 - i, 0, 0, 0))],
        out_shape=[jax.ShapeDtypeStruct((B, TOK_SLABS, T, LANES), jnp.bfloat16),
                   jax.ShapeDtypeStruct((B, KV_SLABS, T, LANES), jnp.bfloat16),
                   jax.ShapeDtypeStruct((B, T // C, RET_HEADS, C, C), jnp.bfloat16)],
        scratch_shapes=[pltpu.VMEM((RET_HEADS, C, C), jnp.float32)],
        compiler_params=pltpu.CompilerParams(dimension_semantics=("arbitrary", "arbitrary"),
                                             vmem_limit_bytes=VMEM_LIMIT_BYTES),
        name="in_proj",
    )(x, mod3, gain, w_in, bd, qg, kg, cosw, sinw, base, dec, cdec)


def _halo_start(i, tt, seq_len):
    units = jnp.clip(i * (tt // HALO_TOKENS) - 1, 0, (seq_len - tt) // HALO_TOKENS - 2)
    return units * HALO_TOKENS


def _mix_out_kernel(x_ref, mod_ref, tok_ref, kv_ref, bias_ref, sb_ref,
                    dmat_ref, dec_ref, cdec_ref, og_ref, wo_ref,
                    y_ref,
                    state_ref, mix_ref, s0_ref, s1_ref, m0_ref, m1_ref,
                    p0_ref, p1_ref, *, rows):
    bf16, f32 = jnp.bfloat16, jnp.float32
    tt = x_ref.shape[0]
    n_rows = tt // GRID_W
    i = pl.program_id(1)

    @pl.when(i == 0)
    def _():
        state_ref[...] = jnp.zeros_like(state_ref)

    first_row = _halo_start(i, tt, rows * GRID_W) // GRID_W

    lane = lax.broadcasted_iota(jnp.int32, (GRID_W, LANES), 1)
    low = lane < NA_HEAD_DIM
    win = NA_WIN_H * GRID_W
    n_blk = win // LANES
    ones = jnp.ones((win, LANES), bf16)

    def window(r):
        grow = i * n_rows + r
        start = jnp.clip(grow - NA_WIN_H // 2, 0, rows - NA_WIN_H)
        w0 = pl.multiple_of((start - first_row) * GRID_W, GRID_W)
        return w0, grow - start

    def na_scores(r, s_ref, m_ref):
        w0, off = window(r)
        q0 = pl.multiple_of(r * GRID_W, GRID_W)
        for p in range(NA_PAIRS):
            q = tok_ref[TOK_QA + p, pl.ds(q0, GRID_W), :]
            q2 = jnp.concatenate([jnp.where(low, q, jnp.zeros_like(q)),
                                  jnp.where(low, jnp.zeros_like(q), q)], axis=0)
            kw = kv_ref[KV_KA + p, pl.ds(w0, win), :]
            s = lax.dot_general(q2, kw, _NT, preferred_element_type=f32)
            for g in range(LANES // NA_GROUP):
                gs = slice(g * NA_GROUP, (g + 1) * NA_GROUP)
                m = None
                for j in range(n_blk):
                    blk = _slab(s[gs], j) + bias_ref[p, 2 * j - off + (NA_WIN_H - 1), gs, :]
                    s_ref[p, j, gs, :] = blk
                    m = blk if m is None else jnp.maximum(m, blk)
                m_ref[p, gs, :] = jnp.broadcast_to(jnp.max(m, axis=-1, keepdims=True),
                                                   (NA_GROUP, LANES))

    def na_probs(s_ref, m_ref, p_ref):
        for p in range(NA_PAIRS):
            for g in range(LANES // NA_GROUP):
                gs = slice(g * NA_GROUP, (g + 1) * NA_GROUP)
                m = m_ref[p, gs, :]
                for j in range(n_blk):
                    p_ref[p, j, gs, :] = jnp.exp2(s_ref[p, j, gs, :] - m).astype(bf16)

    def na_output(r, p_ref):
        w0, _ = window(r)
        q0 = pl.multiple_of(r * GRID_W, GRID_W)
        for p in range(NA_PAIRS):
            vw = jnp.concatenate([kv_ref[KV_VA + p, pl.ds(w0, win), :], ones], axis=1)
            pm = jnp.concatenate([p_ref[p, j] for j in range(n_blk)], axis=1)
            o2 = jnp.dot(pm, vw, preferred_element_type=f32)
            o2 = _slab(o2, 0) / _slab(o2, 1)
            o = jnp.where(low, o2[0:GRID_W], o2[GRID_W:])
            gate = tok_ref[TOK_GA + p, pl.ds(q0, GRID_W), :].astype(f32)
            mix_ref[p, pl.ds(q0, GRID_W), :] = (o * gate).astype(bf16)

    s_refs, m_refs, p_refs = (s0_ref, s1_ref), (m0_ref, m1_ref), (p0_ref, p1_ref)
    for t in range(n_rows + 2):
        if t < n_rows:
            na_scores(t, s_refs[t % 2], m_refs[t % 2])
        if 1 <= t <= n_rows:
            na_probs(s_refs[(t - 1) % 2], m_refs[(t - 1) % 2], p_refs[(t - 1) % 2])
        if t >= 2:
            na_output(t - 2, p_refs[t % 2])

    C = RET_CHUNK
    for c in range(tt // C):
        rws = slice(c * C, (c + 1) * C)
        for h in range(RET_HEADS):
            q = tok_ref[TOK_QR + h, rws, :]
            k = tok_ref[TOK_KR + h, rws, :]
            v = tok_ref[TOK_VR + h, rws, :]
            s = lax.dot_general(q, k, _NT, preferred_element_type=f32) * dmat_ref[h]
            o = jnp.dot(s.astype(bf16), v, preferred_element_type=f32)
            sf = state_ref[h]
            o = o + jnp.dot(q, sf.astype(bf16), preferred_element_type=f32) * dec_ref[DEC_QF, h]
            o = o + jnp.dot(q, sb_ref[c, h], preferred_element_type=f32) * dec_ref[DEC_QB, h]
            kd = (k.astype(f32) * dec_ref[DEC_KF, h]).astype(bf16)
            state_ref[h] = sf * cdec_ref[0, h, 0:1, :] + lax.dot_general(
                kd, v, _TN, preferred_element_type=f32)
            ms = jnp.mean(o * o, axis=-1, keepdims=True)
            rn = o * lax.rsqrt(ms + NORM_EPS) * og_ref[h]
            g = tok_ref[TOK_GR + h, rws, :].astype(f32)
            mix_ref[NA_PAIRS + h, rws, :] = (rn * g).astype(bf16)

    mix = jnp.concatenate([mix_ref[n] for n in range(NA_PAIRS + RET_HEADS)], axis=1)
    out = jnp.dot(mix, wo_ref[...], preferred_element_type=f32)
    y_ref[...] = x_ref[...] + mod_ref[2:3, :] * out


def _mix_out(x, mod3, tok_arr, kv, sb, bias, dmat, dec, cdec, og, w_out):
    B, T, _ = x.shape
    tt = MIX_TILE
    nt = T // tt
    C = RET_CHUNK
    rows = T // GRID_W
    win = NA_WIN_H * GRID_W
    assert T % tt == 0 and T >= tt + 2 * HALO_TOKENS, (T, tt)
    tok = lambda b, i: (b, i, 0)
    c2 = lambda b, i: (0, 0)
    c3 = lambda b, i: (0, 0, 0)
    c4 = lambda b, i: (0, 0, 0, 0)
    halo_spec = pl.BlockSpec(
        (None, pl.Element(KV_SLABS), pl.Element(tt + 2 * HALO_TOKENS), pl.Element(LANES)),
        lambda b, i: (b, 0, _halo_start(i, tt, T), 0))
    score_shape = (NA_PAIRS, win // LANES, LANES, LANES)
    return pl.pallas_call(
        functools.partial(_mix_out_kernel, rows=rows),
        grid=(B, nt),
        in_specs=[pl.BlockSpec((None, tt, D_MODEL), tok),
                  pl.BlockSpec((None, 3, D_MODEL), lambda b, i: (b, 0, 0)),
                  pl.BlockSpec((None, TOK_SLABS, tt, LANES), lambda b, i: (b, 0, i, 0)),
                  halo_spec,
                  pl.BlockSpec((NA_PAIRS, BIAS_SLOTS, LANES, LANES), c4,
                               pipeline_mode=pl.Buffered(1)),
                  pl.BlockSpec((None, tt // C, RET_HEADS, C, C), lambda b, i: (b, i, 0, 0, 0)),
                  pl.BlockSpec((RET_HEADS, C, C), c3),
                  pl.BlockSpec((DEC_TABLES, RET_HEADS, C, LANES), c4),
                  pl.BlockSpec((2, RET_HEADS, 8, LANES), c4),
                  pl.BlockSpec((RET_HEADS, 1, RET_HEAD_DIM), c3),
                  pl.BlockSpec((D_MODEL, D_MODEL), c2, pipeline_mode=pl.Buffered(1))],
        out_specs=pl.BlockSpec((None, tt, D_MODEL), tok),
        out_shape=jax.ShapeDtypeStruct((B, T, D_MODEL), jnp.float32),
        scratch_shapes=[pltpu.VMEM((RET_HEADS, C, C), jnp.float32),
                        pltpu.VMEM((NA_PAIRS + RET_HEADS, tt, LANES), jnp.bfloat16),
                        pltpu.VMEM(score_shape, jnp.float32),
                        pltpu.VMEM(score_shape, jnp.float32),
                        pltpu.VMEM((NA_PAIRS, LANES, LANES), jnp.float32),
                        pltpu.VMEM((NA_PAIRS, LANES, LANES), jnp.float32),
                        pltpu.VMEM(score_shape, jnp.bfloat16),
                        pltpu.VMEM(score_shape, jnp.bfloat16)],
        compiler_params=pltpu.CompilerParams(dimension_semantics=("arbitrary", "arbitrary"),
                                             vmem_limit_bytes=VMEM_LIMIT_BYTES),
        name="mix_out",
    )(x, mod3, tok_arr, kv, bias, sb, dmat, dec, cdec, og, w_out)


def _rope_tables(positions):
    half = RET_HEAD_DIM // 2
    inv = ROPE_BASE ** (-jnp.arange(half, dtype=jnp.float32) / half)
    ang = positions.astype(jnp.float32)[:, None] * inv[None, :]
    cos, sin = jnp.cos(ang), jnp.sin(ang)
    return jnp.concatenate([cos, cos], axis=1), jnp.concatenate([sin, sin], axis=1)


def _layer(x, mod3, shared):
    (gain, w_in, bd, qg, kg, cosw, sinw, base, bias, dmat, dec, cdec, og, w_out) = shared
    nt = x.shape[1] // PROJ_TILE
    tok_arr, kv, sb = _in_proj(x, mod3, gain, w_in, bd, qg, kg, cosw, sinw, base[:nt], dec, cdec)
    return _mix_out(x, mod3, tok_arr, kv, sb, bias, dmat, dec, cdec, og, w_out)


def kernel(x_prompt, x_sample, c_prompt, c_sample, norm_gain, w_ada, b_ada, w_in, na_q_gain,
           na_k_gain, na_rpb, ret_decay_f, ret_decay_b, ret_out_gain, w_out):
    depth = norm_gain.shape[0]
    bp, bs = x_prompt.shape[0], x_sample.shape[0]
    nb = -(-(bp + bs) // 8) * 8
    t_max = max(x_prompt.shape[1], x_sample.shape[1])
    cosw, sinw = _rope_tables(jnp.arange(PROJ_TILE))
    base = jnp.stack(_rope_tables(jnp.arange(t_max // PROJ_TILE) * PROJ_TILE), axis=1)
    hid = np.arange(256) // NA_HEAD_DIM
    bd = jnp.asarray(hid[:, None] == hid[None, :], jnp.bfloat16)
    c_all = jnp.concatenate(
        [c_prompt, c_sample, jnp.zeros((nb - bp - bs, D_MODEL), jnp.float32)], axis=0)
    y_prompt, y_sample = x_prompt, x_sample
    for l in range(depth):
        rpb = jnp.pad(na_rpb[l], ((0, 0), (0, 0), (0, LANES - (2 * NA_WIN_W - 1))))
        mod, dmat, dec, cdec, bias = _prep(c_all, w_ada[l], b_ada[l][None],
                                           ret_decay_f[l], ret_decay_b[l], rpb)
        mod3 = mod.reshape(nb, 3, D_MODEL)
        w_seg = w_in[l].astype(jnp.bfloat16).reshape(D_MODEL, IN_WIDTH // SEG, SEG)
        shared = (norm_gain[l][None], w_seg.transpose(1, 0, 2), bd,
                  jnp.tile(na_q_gain[l], NA_HEADS)[None], jnp.tile(na_k_gain[l], NA_HEADS)[None],
                  cosw, sinw, base, bias, dmat, dec, cdec,
                  ret_out_gain[l].reshape(RET_HEADS, 1, RET_HEAD_DIM),
                  w_out[l].astype(jnp.bfloat16))
        y_prompt = _layer(y_prompt, mod3[:bp], shared)
        y_sample = _layer(y_sample, mod3[bp:bp + bs], shared)
    return (y_prompt, y_sample)
```

```python
import functools

import numpy as np
import jax
import jax.numpy as jnp
from jax import lax
from jax.experimental import pallas as pl
from jax.experimental.pallas import tpu as pltpu

D_MODEL = 1024
GRID_W = 64
NA_HEADS = 8
NA_HEAD_DIM = 64
NA_WIDTH = NA_HEADS * NA_HEAD_DIM
NA_PAIRS = NA_HEADS // 2
NA_WIN_H = 8
NA_WIN_W = 16
RET_HEADS = 4
RET_HEAD_DIM = 128
RET_WIDTH = RET_HEADS * RET_HEAD_DIM
RET_CHUNK = 128
ROPE_BASE = 10000.0
NORM_EPS = 1e-6
LOG2E = 1.4426950408889634
IN_WIDTH = 4 * NA_WIDTH + 4 * RET_WIDTH
SEG = 512
LANES = 128

PROJ_TILE = 1024
MIX_TILE = 1024
HALO_ROWS = NA_WIN_H // 2
HALO_TOKENS = HALO_ROWS * GRID_W
BIAS_SLOTS = 2 * NA_WIN_H - 2
NA_GROUP = 16

TOK_QA, TOK_GA, TOK_QR, TOK_KR, TOK_VR, TOK_GR = 0, 4, 8, 12, 16, 20
TOK_SLABS = 24
KV_KA, KV_VA = 0, NA_PAIRS
KV_SLABS = 2 * NA_PAIRS
DEC_QF, DEC_QB, DEC_KF, DEC_KB = range(4)
DEC_TABLES = 4
VMEM_LIMIT_BYTES = 56 * 1024 * 1024

_NT = (((1,), (1,)), ((), ()))
_TN = (((0,), (0,)), ((), ()))


def _silu(v):
    return v / (1.0 + jnp.exp(-v))


def _slab(v, n):
    return v[:, n * LANES:(n + 1) * LANES]


def _prep_kernel(dec_f_ref, dec_b_ref, c_ref, w_ref, b_ref, rpb_ref,
                 mod_ref, dmat_ref, dec_ref, cdec_ref, bias_ref):
    c = c_ref[...]
    mod_ref[...] = jnp.dot(_silu(c), w_ref[...], preferred_element_type=jnp.float32) + b_ref[...]

    @pl.when(pl.program_id(0) == 0)
    def _():
        C = RET_CHUNK
        pos = lax.broadcasted_iota(jnp.int32, (C, LANES), 0).astype(jnp.float32)
        ri = lax.broadcasted_iota(jnp.int32, (C, C), 0)
        ci = lax.broadcasted_iota(jnp.int32, (C, C), 1)
        diff = (ri - ci).astype(jnp.float32)
        for h in range(RET_HEADS):
            lgf = -jnp.exp(jnp.full((1, LANES), dec_f_ref[h], jnp.float32))
            lgb = -jnp.exp(jnp.full((1, LANES), dec_b_ref[h], jnp.float32))
            dec_ref[DEC_QF, h] = jnp.exp(lgf * (pos + 1.0))
            dec_ref[DEC_QB, h] = jnp.exp(lgb * (C - pos))
            dec_ref[DEC_KF, h] = jnp.exp(lgf * (C - 1.0 - pos))
            dec_ref[DEC_KB, h] = jnp.exp(lgb * pos)
            cdec_ref[0, h] = jnp.broadcast_to(jnp.exp(lgf * C), (8, LANES))
            cdec_ref[1, h] = jnp.broadcast_to(jnp.exp(lgb * C), (8, LANES))
            dmat_ref[h] = jnp.where(diff >= 0, jnp.exp(lgf * jnp.maximum(diff, 0.0)),
                                    jnp.exp(lgb * jnp.maximum(-diff, 0.0)))

        qcol = lax.broadcasted_iota(jnp.int32, (GRID_W, LANES), 0)
        lane = lax.broadcasted_iota(jnp.int32, (GRID_W, LANES), 1)
        kcol = lane % GRID_W
        wstart = jnp.clip(qcol - NA_WIN_W // 2, 0, GRID_W - NA_WIN_W)
        valid = (kcol >= wstart) & (kcol < wstart + NA_WIN_W)
        centre = LANES - (NA_WIN_W - 1)
        for h in range(NA_HEADS):
            for d in range(BIAS_SLOTS):
                even = pltpu.roll(jnp.broadcast_to(rpb_ref[h, d:d + 1, :], (GRID_W, LANES)),
                                  centre, 1, stride=1, stride_axis=0)
                odd = pltpu.roll(jnp.broadcast_to(rpb_ref[h, d + 1:d + 2, :], (GRID_W, LANES)),
                                 (centre + GRID_W) % LANES, 1, stride=1, stride_axis=0)
                blk = jnp.where(valid, jnp.where(lane < GRID_W, even, odd) * LOG2E, -jnp.inf)
                bias_ref[h // 2, d, (h % 2) * GRID_W:(h % 2 + 1) * GRID_W, :] = blk


def _prep(c_all, w_ada, b_ada, dec_f, dec_b, rpb):
    nb = c_all.shape[0]
    C = RET_CHUNK
    f32 = jnp.float32
    smem = pl.BlockSpec(memory_space=pltpu.SMEM)
    const2 = lambda j: (0, 0)
    const3 = lambda j: (0, 0, 0)
    const4 = lambda j: (0, 0, 0, 0)
    return pl.pallas_call(
        _prep_kernel,
        grid=(3,),
        in_specs=[smem, smem,
                  pl.BlockSpec((nb, D_MODEL), const2),
                  pl.BlockSpec((D_MODEL, D_MODEL), lambda j: (0, j)),
                  pl.BlockSpec((1, D_MODEL), lambda j: (0, j)),
                  pl.BlockSpec((NA_HEADS, 2 * NA_WIN_H - 1, LANES), const3)],
        out_specs=[pl.BlockSpec((nb, D_MODEL), lambda j: (0, j)),
                   pl.BlockSpec((RET_HEADS, C, C), const3),
                   pl.BlockSpec((DEC_TABLES, RET_HEADS, C, LANES), const4),
                   pl.BlockSpec((2, RET_HEADS, 8, LANES), const4),
                   pl.BlockSpec((NA_PAIRS, BIAS_SLOTS, LANES, LANES), const4)],
        out_shape=[jax.ShapeDtypeStruct((nb, 3 * D_MODEL), f32),
                   jax.ShapeDtypeStruct((RET_HEADS, C, C), f32),
                   jax.ShapeDtypeStruct((DEC_TABLES, RET_HEADS, C, LANES), f32),
                   jax.ShapeDtypeStruct((2, RET_HEADS, 8, LANES), f32),
                   jax.ShapeDtypeStruct((NA_PAIRS, BIAS_SLOTS, LANES, LANES), f32)],
        compiler_params=pltpu.CompilerParams(dimension_semantics=("arbitrary",),
                                             vmem_limit_bytes=VMEM_LIMIT_BYTES),
        name="prep",
    )(dec_f, dec_b, c_all, w_ada, b_ada, rpb)


def _in_proj_kernel(x_ref, mod_ref, gain_ref, w_ref, bd_ref, qg_ref, kg_ref, cos_ref, sin_ref,
                    base_ref, dec_ref, cdec_ref,
                    tok_ref, kv_ref, sb_ref,
                    state_ref):
    bf16, f32 = jnp.bfloat16, jnp.float32
    tt = x_ref.shape[0]

    @pl.when(pl.program_id(1) == 0)
    def _():
        state_ref[...] = jnp.zeros_like(state_ref)

    x = x_ref[...]
    ms = jnp.mean(x * x, axis=-1, keepdims=True)
    a = gain_ref[...] * (1.0 + mod_ref[1:2, :])
    hb = (x * lax.rsqrt(ms + NORM_EPS) * a + mod_ref[0:1, :]).astype(bf16)

    def seg(s):
        return jnp.dot(hb, w_ref[s], preferred_element_type=f32)

    def head_norm(p, g):
        pp = (p * p).astype(bf16)
        ss = jnp.concatenate(
            [jnp.dot(pp[:, c * 256:(c + 1) * 256], bd_ref[...], preferred_element_type=f32)
             for c in range(NA_WIDTH // 256)], axis=1)
        return p * lax.rsqrt(ss * (1.0 / NA_HEAD_DIM) + NORM_EPS) * g

    def put(ref, first, val):
        for n in range(val.shape[1] // LANES):
            ref[first + n] = _slab(val, n)

    put(tok_ref, TOK_QA,
        head_norm(seg(0), qg_ref[...] * (NA_HEAD_DIM ** -0.5 * LOG2E)).astype(bf16))
    put(kv_ref, KV_KA, head_norm(seg(1), kg_ref[...]).astype(bf16))
    put(kv_ref, KV_VA, seg(2).astype(bf16))
    put(tok_ref, TOK_GA, _silu(seg(3)).astype(bf16))

    c0, s0 = base_ref[0:1, :], base_ref[1:2, :]
    cw, sw = cos_ref[...], sin_ref[...]
    hlane = lax.broadcasted_iota(jnp.int32, (1, RET_HEAD_DIM), 1)
    cosf = c0 * cw - s0 * sw
    sinf = (s0 * cw + c0 * sw) * jnp.where(hlane < RET_HEAD_DIM // 2, -1.0, 1.0)

    def rotary(p):
        outs = []
        for h in range(RET_HEADS):
            ph = _slab(p, h)
            outs.append(ph * cosf + pltpu.roll(ph, RET_HEAD_DIM // 2, 1) * sinf)
        return jnp.concatenate(outs, axis=1)

    put(tok_ref, TOK_QR, rotary(seg(4)).astype(bf16))
    kr = rotary(seg(5)) * (RET_HEAD_DIM ** -0.5)
    put(tok_ref, TOK_KR, kr.astype(bf16))
    vb = seg(6).astype(bf16)
    put(tok_ref, TOK_VR, vb)
    put(tok_ref, TOK_GR, _silu(seg(7)).astype(bf16))

    C = RET_CHUNK
    for c in reversed(range(tt // C)):
        rows = slice(c * C, (c + 1) * C)
        for h in range(RET_HEADS):
            s_old = state_ref[h]
            sb_ref[c, h] = s_old.astype(bf16)
            kd = (_slab(kr[rows], h) * dec_ref[DEC_KB, h]).astype(bf16)
            kv = lax.dot_general(kd, _slab(vb[rows], h), _TN, preferred_element_type=f32)
            state_ref[h] = s_old * cdec_ref[1, h, 0:1, :] + kv


def _in_proj(x, mod3, gain, w_in, bd, qg, kg, cosw, sinw, base, dec, cdec):
    B, T, _ = x.shape
    tt = PROJ_TILE
    nt = T // tt
    C = RET_CHUNK
    rev = lambda b, i: (b, nt - 1 - i, 0)
    rev4 = lambda b, i: (b, 0, nt - 1 - i, 0)
    c2 = lambda b, i: (0, 0)
    c3 = lambda b, i: (0, 0, 0)
    c4 = lambda b, i: (0, 0, 0, 0)
    return pl.pallas_call(
        _in_proj_kernel,
        grid=(B, nt),
        in_specs=[pl.BlockSpec((None, tt, D_MODEL), rev),
                  pl.BlockSpec((None, 3, D_MODEL), lambda b, i: (b, 0, 0)),
                  pl.BlockSpec((1, D_MODEL), c2),
                  pl.BlockSpec((IN_WIDTH // SEG, D_MODEL, SEG), c3, pipeline_mode=pl.Buffered(1)),
                  pl.BlockSpec((256, 256), c2),
                  pl.BlockSpec((1, NA_WIDTH), c2),
                  pl.BlockSpec((1, NA_WIDTH), c2),
                  pl.BlockSpec((tt, RET_HEAD_DIM), c2),
                  pl.BlockSpec((tt, RET_HEAD_DIM), c2),
                  pl.BlockSpec((None, 2, RET_HEAD_DIM), lambda b, i: (nt - 1 - i, 0, 0)),
                  pl.BlockSpec((DEC_TABLES, RET_HEADS, C, LANES), c4),
                  pl.BlockSpec((2, RET_HEADS, 8, LANES), c4)],
        out_specs=[pl.BlockSpec((None, TOK_SLABS, tt, LANES), rev4),
                   pl.BlockSpec((None, KV_SLABS, tt, LANES), rev4),
                   pl.BlockSpec((None, tt // C, RET_HEADS, C, C),
                                lambda b, i: (b, nt - 1 - i, 0, 0, 0))],
        out_shape=[jax.ShapeDtypeStruct((B, TOK_SLABS, T, LANES), jnp.bfloat16),
                   jax.ShapeDtypeStruct((B, KV_SLABS, T, LANES), jnp.bfloat16),
                   jax.ShapeDtypeStruct((B, T // C, RET_HEADS, C, C), jnp.bfloat16)],
        scratch_shapes=[pltpu.VMEM((RET_HEADS, C, C), jnp.float32)],
        compiler_params=pltpu.CompilerParams(dimension_semantics=("arbitrary", "arbitrary"),
                                             vmem_limit_bytes=VMEM_LIMIT_BYTES),
        name="in_proj",
    )(x, mod3, gain, w_in, bd, qg, kg, cosw, sinw, base, dec, cdec)


def _halo_start(i, tt, seq_len):
    units = jnp.clip(i * (tt // HALO_TOKENS) - 1, 0, (seq_len - tt) // HALO_TOKENS - 2)
    return units * HALO_TOKENS


def _mix_out_kernel(x_ref, mod_ref, tok_ref, kv_ref, bias_ref, sb_ref,
                    dmat_ref, dec_ref, cdec_ref, og_ref, wo_ref,
                    y_ref,
                    state_ref, mix_ref, s0_ref, s1_ref, m0_ref, m1_ref,
                    p0_ref, p1_ref, *, rows):
    bf16, f32 = jnp.bfloat16, jnp.float32
    tt = x_ref.shape[0]
    n_rows = tt // GRID_W
    i = pl.program_id(1)

    @pl.when(i == 0)
    def _():
        state_ref[...] = jnp.zeros_like(state_ref)

    first_row = _halo_start(i, tt, rows * GRID_W) // GRID_W

    lane = lax.broadcasted_iota(jnp.int32, (GRID_W, LANES), 1)
    low = lane < NA_HEAD_DIM
    win = NA_WIN_H * GRID_W
    n_blk = win // LANES
    ones = jnp.ones((win, LANES), bf16)

    def window(r):
        grow = i * n_rows + r
        start = jnp.clip(grow - NA_WIN_H // 2, 0, rows - NA_WIN_H)
        w0 = pl.multiple_of((start - first_row) * GRID_W, GRID_W)
        return w0, grow - start

    def na_scores(r, s_ref, m_ref):
        w0, off = window(r)
        q0 = pl.multiple_of(r * GRID_W, GRID_W)
        for p in range(NA_PAIRS):
            q = tok_ref[TOK_QA + p, pl.ds(q0, GRID_W), :]
            q2 = jnp.concatenate([jnp.where(low, q, jnp.zeros_like(q)),
                                  jnp.where(low, jnp.zeros_like(q), q)], axis=0)
            kw = kv_ref[KV_KA + p, pl.ds(w0, win), :]
            s = lax.dot_general(q2, kw, _NT, preferred_element_type=f32)
            for g in range(LANES // NA_GROUP):
                gs = slice(g * NA_GROUP, (g + 1) * NA_GROUP)
                m = None
                for j in range(n_blk):
                    blk = _slab(s[gs], j) + bias_ref[p, 2 * j - off + (NA_WIN_H - 1), gs, :]
                    s_ref[p, j, gs, :] = blk
                    m = blk if m is None else jnp.maximum(m, blk)
                m_ref[p, gs, :] = jnp.broadcast_to(jnp.max(m, axis=-1, keepdims=True),
                                                   (NA_GROUP, LANES))

    def na_probs(s_ref, m_ref, p_ref):
        for p in range(NA_PAIRS):
            for g in range(LANES // NA_GROUP):
                gs = slice(g * NA_GROUP, (g + 1) * NA_GROUP)
                m = m_ref[p, gs, :]
                for j in range(n_blk):
                    p_ref[p, j, gs, :] = jnp.exp2(s_ref[p, j, gs, :] - m).astype(bf16)

    def na_output(r, p_ref):
        w0, _ = window(r)
        q0 = pl.multiple_of(r * GRID_W, GRID_W)
        for p in range(NA_PAIRS):
            vw = jnp.concatenate([kv_ref[KV_VA + p, pl.ds(w0, win), :], ones], axis=1)
            pm = jnp.concatenate([p_ref[p, j] for j in range(n_blk)], axis=1)
            o2 = jnp.dot(pm, vw, preferred_element_type=f32)
            num = jnp.where(low, o2[0:GRID_W, 0:LANES], o2[GRID_W:, 0:LANES])
            den = jnp.where(low, o2[0:GRID_W, LANES:], o2[GRID_W:, LANES:])
            o = num / den
            gate = tok_ref[TOK_GA + p, pl.ds(q0, GRID_W), :]
            mix_ref[p, pl.ds(q0, GRID_W), :] = (o.astype(bf16) * gate)

    s_refs, m_refs, p_refs = (s0_ref, s1_ref), (m0_ref, m1_ref), (p0_ref, p1_ref)
    for t in range(n_rows + 2):
        if t < n_rows:
            na_scores(t, s_refs[t % 2], m_refs[t % 2])
        if 1 <= t <= n_rows:
            na_probs(s_refs[(t - 1) % 2], m_refs[(t - 1) % 2], p_refs[(t - 1) % 2])
        if t >= 2:
            na_output(t - 2, p_refs[t % 2])

    C = RET_CHUNK
    for c in range(tt // C):
        rws = slice(c * C, (c + 1) * C)
        for h in range(RET_HEADS):
            q = tok_ref[TOK_QR + h, rws, :]
            k = tok_ref[TOK_KR + h, rws, :]
            v = tok_ref[TOK_VR + h, rws, :]
            s = lax.dot_general(q, k, _NT, preferred_element_type=f32) * dmat_ref[h]
            o = jnp.dot(s.astype(bf16), v, preferred_element_type=f32)
            sf = state_ref[h]
            o = o + jnp.dot(q, sf.astype(bf16), preferred_element_type=f32) * dec_ref[DEC_QF, h]
            o = o + jnp.dot(q, sb_ref[c, h], preferred_element_type=f32) * dec_ref[DEC_QB, h]
            kd = (k.astype(f32) * dec_ref[DEC_KF, h]).astype(bf16)
            state_ref[h] = sf * cdec_ref[0, h, 0:1, :] + lax.dot_general(
                kd, v, _TN, preferred_element_type=f32)
            ms = jnp.mean(o * o, axis=-1, keepdims=True)
            rn = o * lax.rsqrt(ms + NORM_EPS) * og_ref[h]
            g = tok_ref[TOK_GR + h, rws, :]
            mix_ref[NA_PAIRS + h, rws, :] = (rn.astype(bf16) * g)

    mix = jnp.concatenate([mix_ref[n] for n in range(NA_PAIRS + RET_HEADS)], axis=1)
    out = jnp.dot(mix, wo_ref[...], preferred_element_type=f32)
    y_ref[...] = x_ref[...] + mod_ref[2:3, :] * out


def _mix_out(x, mod3, tok_arr, kv, sb, bias, dmat, dec, cdec, og, w_out):
    B, T, _ = x.shape
    tt = MIX_TILE
    nt = T // tt
    C = RET_CHUNK
    rows = T // GRID_W
    win = NA_WIN_H * GRID_W
    assert T % tt == 0 and T >= tt + 2 * HALO_TOKENS, (T, tt)
    tok = lambda b, i: (b, i, 0)
    c2 = lambda b, i: (0, 0)
    c3 = lambda b, i: (0, 0, 0)
    c4 = lambda b, i: (0, 0, 0, 0)
    halo_spec = pl.BlockSpec(
        (None, pl.Element(KV_SLABS), pl.Element(tt + 2 * HALO_TOKENS), pl.Element(LANES)),
        lambda b, i: (b, 0, _halo_start(i, tt, T), 0))
    score_shape = (NA_PAIRS, win // LANES, LANES, LANES)
    return pl.pallas_call(
        functools.partial(_mix_out_kernel, rows=rows),
        grid=(B, nt),
        in_specs=[pl.BlockSpec((None, tt, D_MODEL), tok),
                  pl.BlockSpec((None, 3, D_MODEL), lambda b, i: (b, 0, 0)),
                  pl.BlockSpec((None, TOK_SLABS, tt, LANES), lambda b, i: (b, 0, i, 0)),
                  halo_spec,
                  pl.BlockSpec((NA_PAIRS, BIAS_SLOTS, LANES, LANES), c4,
                               pipeline_mode=pl.Buffered(1)),
                  pl.BlockSpec((None, tt // C, RET_HEADS, C, C), lambda b, i: (b, i, 0, 0, 0)),
                  pl.BlockSpec((RET_HEADS, C, C), c3),
                  pl.BlockSpec((DEC_TABLES, RET_HEADS, C, LANES), c4),
                  pl.BlockSpec((2, RET_HEADS, 8, LANES), c4),
                  pl.BlockSpec((RET_HEADS, 1, RET_HEAD_DIM), c3),
                  pl.BlockSpec((D_MODEL, D_MODEL), c2, pipeline_mode=pl.Buffered(1))],
        out_specs=pl.BlockSpec((None, tt, D_MODEL), tok),
        out_shape=jax.ShapeDtypeStruct((B, T, D_MODEL), jnp.float32),
        scratch_shapes=[pltpu.VMEM((RET_HEADS, C, C), jnp.float32),
                        pltpu.VMEM((NA_PAIRS + RET_HEADS, tt, LANES), jnp.bfloat16),
                        pltpu.VMEM(score_shape, jnp.float32),
                        pltpu.VMEM(score_shape, jnp.float32),
                        pltpu.VMEM((NA_PAIRS, LANES, LANES), jnp.float32),
                        pltpu.VMEM((NA_PAIRS, LANES, LANES), jnp.float32),
                        pltpu.VMEM(score_shape, jnp.bfloat16),
                        pltpu.VMEM(score_shape, jnp.bfloat16)],
        compiler_params=pltpu.CompilerParams(dimension_semantics=("arbitrary", "arbitrary"),
                                             vmem_limit_bytes=VMEM_LIMIT_BYTES),
        name="mix_out",
    )(x, mod3, tok_arr, kv, bias, sb, dmat, dec, cdec, og, w_out)


def _rope_tables(positions):
    half = RET_HEAD_DIM // 2
    inv = ROPE_BASE ** (-jnp.arange(half, dtype=jnp.float32) / half)
    ang = positions.astype(jnp.float32)[:, None] * inv[None, :]
    cos, sin = jnp.cos(ang), jnp.sin(ang)
    return jnp.concatenate([cos, cos], axis=1), jnp.concatenate([sin, sin], axis=1)


def _layer(x, mod3, shared):
    (gain, w_in, bd, qg, kg, cosw, sinw, base, bias, dmat, dec, cdec, og, w_out) = shared
    nt = x.shape[1] // PROJ_TILE
    tok_arr, kv, sb = _in_proj(x, mod3, gain, w_in, bd, qg, kg, cosw, sinw, base[:nt], dec, cdec)
    return _mix_out(x, mod3, tok_arr, kv, sb, bias, dmat, dec, cdec, og, w_out)


def kernel(x_prompt, x_sample, c_prompt, c_sample, norm_gain, w_ada, b_ada, w_in, na_q_gain,
           na_k_gain, na_rpb, ret_decay_f, ret_decay_b, ret_out_gain, w_out):
    depth = norm_gain.shape[0]
    bp, bs = x_prompt.shape[0], x_sample.shape[0]
    nb = -(-(bp + bs) // 8) * 8
    t_max = max(x_prompt.shape[1], x_sample.shape[1])
    cosw, sinw = _rope_tables(jnp.arange(PROJ_TILE))
    base = jnp.stack(_rope_tables(jnp.arange(t_max // PROJ_TILE) * PROJ_TILE), axis=1)
    hid = np.arange(256) // NA_HEAD_DIM
    bd = jnp.asarray(hid[:, None] == hid[None, :], jnp.bfloat16)
    c_all = jnp.concatenate(
        [c_prompt, c_sample, jnp.zeros((nb - bp - bs, D_MODEL), jnp.float32)], axis=0)
    y_prompt, y_sample = x_prompt, x_sample
    for l in range(depth):
        rpb = jnp.pad(na_rpb[l], ((0, 0), (0, 0), (0, LANES - (2 * NA_WIN_W - 1))))
        mod, dmat, dec, cdec, bias = _prep(c_all, w_ada[l], b_ada[l][None],
                                           ret_decay_f[l], ret_decay_b[l], rpb)
        mod3 = mod.reshape(nb, 3, D_MODEL)
        w_seg = w_in[l].astype(jnp.bfloat16).reshape(D_MODEL, IN_WIDTH // SEG, SEG)
        shared = (norm_gain[l][None], w_seg.transpose(1, 0, 2), bd,
                  jnp.tile(na_q_gain[l], NA_HEADS)[None], jnp.tile(na_k_gain[l], NA_HEADS)[None],
                  cosw, sinw, base, bias, dmat, dec, cdec,
                  ret_out_gain[l].reshape(RET_HEADS, 1, RET_HEAD_DIM),
                  w_out[l].astype(jnp.bfloat16))
        y_prompt = _layer(y_prompt, mod3[:bp], shared)
        y_sample = _layer(y_sample, mod3[bp:bp + bs], shared)
    return (y_prompt, y_sample)
```

```python
import functools

import numpy as np
import jax
import jax.numpy as jnp
from jax import lax
from jax.experimental import pallas as pl
from jax.experimental.pallas import tpu as pltpu

D_MODEL = 1024
GRID_W = 64
NA_HEADS = 8
NA_HEAD_DIM = 64
NA_WIDTH = NA_HEADS * NA_HEAD_DIM
NA_PAIRS = NA_HEADS // 2
NA_WIN_H = 8
NA_WIN_W = 16
RET_HEADS = 4
RET_HEAD_DIM = 128
RET_WIDTH = RET_HEADS * RET_HEAD_DIM
RET_CHUNK = 128
ROPE_BASE = 10000.0
NORM_EPS = 1e-6
LOG2E = 1.4426950408889634
IN_WIDTH = 4 * NA_WIDTH + 4 * RET_WIDTH
SEG = 512
LANES = 128

PROJ_TILE = 1024
MIX_TILE = 1024
HALO_ROWS = NA_WIN_H // 2
HALO_TOKENS = HALO_ROWS * GRID_W
HALF_W = GRID_W // 2
KEY_GROUP = LANES // HALF_W
BIAS_SLOTS = 2 * NA_WIN_H - KEY_GROUP
NA_GROUP = 16

TOK_QA, TOK_GA, TOK_QR, TOK_KR, TOK_VR, TOK_GR = 0, 4, 8, 12, 16, 20
TOK_SLABS = 24
KV_KA, KV_VA = 0, NA_PAIRS
KV_SLABS = 2 * NA_PAIRS
DEC_QF, DEC_QB, DEC_KF, DEC_KB = range(4)
DEC_TABLES = 4
VMEM_LIMIT_BYTES = 56 * 1024 * 1024

_NT = (((1,), (1,)), ((), ()))
_TN = (((0,), (0,)), ((), ()))


def _silu(v):
    return v / (1.0 + jnp.exp(-v))


def _slab(v, n):
    return v[:, n * LANES:(n + 1) * LANES]


def _prep_kernel(dec_f_ref, dec_b_ref, c_ref, w_ref, b_ref, rpb_ref,
                 mod_ref, dmat_ref, dec_ref, cdec_ref, bias_ref):
    c = c_ref[...]
    mod_ref[...] = jnp.dot(_silu(c), w_ref[...], preferred_element_type=jnp.float32) + b_ref[...]

    @pl.when(pl.program_id(0) == 0)
    def _():
        C = RET_CHUNK
        pos = lax.broadcasted_iota(jnp.int32, (C, LANES), 0).astype(jnp.float32)
        ri = lax.broadcasted_iota(jnp.int32, (C, C), 0)
        ci = lax.broadcasted_iota(jnp.int32, (C, C), 1)
        diff = (ri - ci).astype(jnp.float32)
        for h in range(RET_HEADS):
            lgf = -jnp.exp(jnp.full((1, LANES), dec_f_ref[h], jnp.float32))
            lgb = -jnp.exp(jnp.full((1, LANES), dec_b_ref[h], jnp.float32))
            dec_ref[DEC_QF, h] = jnp.exp(lgf * (pos + 1.0))
            dec_ref[DEC_QB, h] = jnp.exp(lgb * (C - pos))
            dec_ref[DEC_KF, h] = jnp.exp(lgf * (C - 1.0 - pos))
            dec_ref[DEC_KB, h] = jnp.exp(lgb * pos)
            cdec_ref[0, h] = jnp.broadcast_to(jnp.exp(lgf * C), (8, LANES))
            cdec_ref[1, h] = jnp.broadcast_to(jnp.exp(lgb * C), (8, LANES))
            dmat_ref[h] = jnp.where(diff >= 0, jnp.exp(lgf * jnp.maximum(diff, 0.0)),
                                    jnp.exp(lgb * jnp.maximum(-diff, 0.0)))

        qcol = lax.broadcasted_iota(jnp.int32, (GRID_W, LANES), 0)
        lane = lax.broadcasted_iota(jnp.int32, (GRID_W, LANES), 1)
        wstart = jnp.clip(qcol - NA_WIN_W // 2, 0, GRID_W - NA_WIN_W)
        for h in range(NA_HEADS):
            for half in range(2):
                kcol = half * HALF_W + lane % HALF_W
                valid = (kcol >= wstart) & (kcol < wstart + NA_WIN_W)
                for d in range(BIAS_SLOTS):
                    blk = None
                    for kr in range(KEY_GROUP):
                        shift = (kr * HALF_W - half * HALF_W - (NA_WIN_W - 1)) % LANES
                        rolled = pltpu.roll(
                            jnp.broadcast_to(rpb_ref[h, d + kr:d + kr + 1, :], (GRID_W, LANES)),
                            shift, 1, stride=1, stride_axis=0)
                        blk = rolled if blk is None else jnp.where(lane >= kr * HALF_W, rolled, blk)
                    bias_ref[h // 2, half, d, (h % 2) * GRID_W:(h % 2 + 1) * GRID_W, :] = (
                        jnp.where(valid, blk * LOG2E, -jnp.inf))


def _prep(c_all, w_ada, b_ada, dec_f, dec_b, rpb):
    nb = c_all.shape[0]
    C = RET_CHUNK
    f32 = jnp.float32
    smem = pl.BlockSpec(memory_space=pltpu.SMEM)
    const2 = lambda j: (0, 0)
    const3 = lambda j: (0, 0, 0)
    const4 = lambda j: (0, 0, 0, 0)
    return pl.pallas_call(
        _prep_kernel,
        grid=(3,),
        in_specs=[smem, smem,
                  pl.BlockSpec((nb, D_MODEL), const2),
                  pl.BlockSpec((D_MODEL, D_MODEL), lambda j: (0, j)),
                  pl.BlockSpec((1, D_MODEL), lambda j: (0, j)),
                  pl.BlockSpec((NA_HEADS, 2 * NA_WIN_H - 1, LANES), const3)],
        out_specs=[pl.BlockSpec((nb, D_MODEL), lambda j: (0, j)),
                   pl.BlockSpec((RET_HEADS, C, C), const3),
                   pl.BlockSpec((DEC_TABLES, RET_HEADS, C, LANES), const4),
                   pl.BlockSpec((2, RET_HEADS, 8, LANES), const4),
                   pl.BlockSpec((NA_PAIRS, 2, BIAS_SLOTS, LANES, LANES),
                                lambda j: (0, 0, 0, 0, 0))],
        out_shape=[jax.ShapeDtypeStruct((nb, 3 * D_MODEL), f32),
                   jax.ShapeDtypeStruct((RET_HEADS, C, C), f32),
                   jax.ShapeDtypeStruct((DEC_TABLES, RET_HEADS, C, LANES), f32),
                   jax.ShapeDtypeStruct((2, RET_HEADS, 8, LANES), f32),
                   jax.ShapeDtypeStruct((NA_PAIRS, 2, BIAS_SLOTS, LANES, LANES), f32)],
        compiler_params=pltpu.CompilerParams(dimension_semantics=("arbitrary",),
                                             vmem_limit_bytes=VMEM_LIMIT_BYTES),
        name="prep",
    )(dec_f, dec_b, c_all, w_ada, b_ada, rpb)


def _in_proj_kernel(x_ref, mod_ref, gain_ref, w_ref, bd_ref, qg_ref, kg_ref, cos_ref, sin_ref,
                    base_ref, dec_ref, cdec_ref,
                    tok_ref, kv_ref, sb_ref,
                    state_ref):
    bf16, f32 = jnp.bfloat16, jnp.float32
    tt = x_ref.shape[0]

    @pl.when(pl.program_id(1) == 0)
    def _():
        state_ref[...] = jnp.zeros_like(state_ref)

    x = x_ref[...]
    ms = jnp.mean(x * x, axis=-1, keepdims=True)
    a = gain_ref[...] * (1.0 + mod_ref[1:2, :])
    hb = (x * lax.rsqrt(ms + NORM_EPS) * a + mod_ref[0:1, :]).astype(bf16)

    def seg(s):
        return jnp.dot(hb, w_ref[s], preferred_element_type=f32)

    def head_norm(p, g):
        pp = (p * p).astype(bf16)
        ss = jnp.concatenate(
            [jnp.dot(pp[:, c * 256:(c + 1) * 256], bd_ref[...], preferred_element_type=f32)
             for c in range(NA_WIDTH // 256)], axis=1)
        return p * lax.rsqrt(ss * (1.0 / NA_HEAD_DIM) + NORM_EPS) * g

    def put(ref, first, val):
        for n in range(val.shape[1] // LANES):
            ref[first + n] = _slab(val, n)

    put(tok_ref, TOK_QA,
        head_norm(seg(0), qg_ref[...] * (NA_HEAD_DIM ** -0.5 * LOG2E)).astype(bf16))
    put(kv_ref, KV_KA, head_norm(seg(1), kg_ref[...]).astype(bf16))
    put(kv_ref, KV_VA, seg(2).astype(bf16))
    put(tok_ref, TOK_GA, _silu(seg(3)).astype(bf16))

    c0, s0 = base_ref[0:1, :], base_ref[1:2, :]
    cw, sw = cos_ref[...], sin_ref[...]
    hlane = lax.broadcasted_iota(jnp.int32, (1, RET_HEAD_DIM), 1)
    cosf = c0 * cw - s0 * sw
    sinf = (s0 * cw + c0 * sw) * jnp.where(hlane < RET_HEAD_DIM // 2, -1.0, 1.0)

    def rotary(p):
        outs = []
        for h in range(RET_HEADS):
            ph = _slab(p, h)
            outs.append(ph * cosf + pltpu.roll(ph, RET_HEAD_DIM // 2, 1) * sinf)
        return jnp.concatenate(outs, axis=1)

    put(tok_ref, TOK_QR, rotary(seg(4)).astype(bf16))
    kr = rotary(seg(5)) * (RET_HEAD_DIM ** -0.5)
    put(tok_ref, TOK_KR, kr.astype(bf16))
    vb = seg(6).astype(bf16)
    put(tok_ref, TOK_VR, vb)
    put(tok_ref, TOK_GR, _silu(seg(7)).astype(bf16))

    C = RET_CHUNK
    for c in reversed(range(tt // C)):
        rows = slice(c * C, (c + 1) * C)
        for h in range(RET_HEADS):
            s_old = state_ref[h]
            sb_ref[c, h] = s_old.astype(bf16)
            kd = (_slab(kr[rows], h) * dec_ref[DEC_KB, h]).astype(bf16)
            kv = lax.dot_general(kd, _slab(vb[rows], h), _TN, preferred_element_type=f32)
            state_ref[h] = s_old * cdec_ref[1, h, 0:1, :] + kv


def _in_proj(x, mod3, gain, w_in, bd, qg, kg, cosw, sinw, base, dec, cdec):
    B, T, _ = x.shape
    tt = PROJ_TILE
    nt = T // tt
    C = RET_CHUNK
    rev = lambda b, i: (b, nt - 1 - i, 0)
    rev4 = lambda b, i: (b, 0, nt - 1 - i, 0)
    c2 = lambda b, i: (0, 0)
    c3 = lambda b, i: (0, 0, 0)
    c4 = lambda b, i: (0, 0, 0, 0)
    return pl.pallas_call(
        _in_proj_kernel,
        grid=(B, nt),
        in_specs=[pl.BlockSpec((None, tt, D_MODEL), rev),
                  pl.BlockSpec((None, 3, D_MODEL), lambda b, i: (b, 0, 0)),
                  pl.BlockSpec((1, D_MODEL), c2),
                  pl.BlockSpec((IN_WIDTH // SEG, D_MODEL, SEG), c3, pipeline_mode=pl.Buffered(1)),
                  pl.BlockSpec((256, 256), c2),
                  pl.BlockSpec((1, NA_WIDTH), c2),
                  pl.BlockSpec((1, NA_WIDTH), c2),
                  pl.BlockSpec((tt, RET_HEAD_DIM), c2),
                  pl.BlockSpec((tt, RET_HEAD_DIM), c2),
                  pl.BlockSpec((None, 2, RET_HEAD_DIM), lambda b, i: (nt - 1 - i, 0, 0)),
                  pl.BlockSpec((DEC_TABLES, RET_HEADS, C, LANES), c4),
                  pl.BlockSpec((2, RET_HEADS, 8, LANES), c4)],
        out_specs=[pl.BlockSpec((None, TOK_SLABS, tt, LANES), rev4),
                   pl.BlockSpec((None, KV_SLABS, tt, LANES), rev4),
                   pl.BlockSpec((None, tt // C, RET_HEADS, C, C),
                                lambda b, i: (b, nt - 1 - i, 0, 0, 0))],
        out_shape=[jax.ShapeDtypeStruct((B, TOK_SLABS, T, LANES), jnp.bfloat16),
                   jax.ShapeDtypeStruct((B, KV_SLABS, T, LANES), jnp.bfloat16),
                   jax.ShapeDtypeStruct((B, T // C, RET_HEADS, C, C), jnp.bfloat16)],
        scratch_shapes=[pltpu.VMEM((RET_HEADS, C, C), jnp.float32)],
        compiler_params=pltpu.CompilerParams(dimension_semantics=("arbitrary", "arbitrary"),
                                             vmem_limit_bytes=VMEM_LIMIT_BYTES),
        name="in_proj",
    )(x, mod3, gain, w_in, bd, qg, kg, cosw, sinw, base, dec, cdec)


def _halo_start(i, tt, seq_len):
    units = jnp.clip(i * (tt // HALO_TOKENS) - 1, 0, (seq_len - tt) // HALO_TOKENS - 2)
    return units * HALO_TOKENS


def _mix_out_kernel(x_ref, mod_ref, tok_ref, kv_ref, bias_ref, sb_ref,
                    dmat_ref, dec_ref, cdec_ref, og_ref, wo_ref,
                    y_ref,
                    state_ref, mix_ref, s0_ref, s1_ref, m0_ref, m1_ref,
                    p0_ref, p1_ref, *, rows):
    bf16, f32 = jnp.bfloat16, jnp.float32
    tt = x_ref.shape[0]
    n_rows = tt // GRID_W
    i = pl.program_id(1)

    @pl.when(i == 0)
    def _():
        state_ref[...] = jnp.zeros_like(state_ref)

    first_row = _halo_start(i, tt, rows * GRID_W) // GRID_W

    lane = lax.broadcasted_iota(jnp.int32, (GRID_W, LANES), 1)
    low = lane < NA_HEAD_DIM
    win = NA_WIN_H * GRID_W
    n_blk = win // LANES
    ones = jnp.ones((win, LANES), bf16)

    def window(r):
        grow = i * n_rows + r
        start = jnp.clip(grow - NA_WIN_H // 2, 0, rows - NA_WIN_H)
        w0 = pl.multiple_of((start - first_row) * GRID_W, GRID_W)
        return w0, grow - start

    def window_keys(slab, w0):
        parts = []
        for j in range(n_blk):
            for kr in range(KEY_GROUP):
                start = w0 + ((j // 2) * KEY_GROUP + kr) * GRID_W + (j % 2) * HALF_W
                parts.append(kv_ref[slab, pl.ds(pl.multiple_of(start, HALF_W), HALF_W), :])
        return jnp.concatenate(parts, axis=0)

    def halves_needed(g):
        q_lo = (g * NA_GROUP) % GRID_W
        q_hi = q_lo + NA_GROUP - 1
        c_lo = min(max(q_lo - NA_WIN_W // 2, 0), GRID_W - NA_WIN_W)
        c_hi = min(max(q_hi - NA_WIN_W // 2, 0), GRID_W - NA_WIN_W) + NA_WIN_W - 1
        return [half for half in range(2)
                if c_lo < (half + 1) * HALF_W and c_hi >= half * HALF_W]

    groups = [(g, slice(g * NA_GROUP, (g + 1) * NA_GROUP),
               [j for j in range(n_blk) if j % 2 in halves_needed(g)])
              for g in range(LANES // NA_GROUP)]

    @pl.when((pl.program_id(0) == 0) & (i == 0))
    def _():
        p0_ref[...] = jnp.zeros_like(p0_ref)
        p1_ref[...] = jnp.zeros_like(p1_ref)

    def na_scores(r, s_ref, m_ref):
        w0, off = window(r)
        q0 = pl.multiple_of(r * GRID_W, GRID_W)
        for p in range(NA_PAIRS):
            q = tok_ref[TOK_QA + p, pl.ds(q0, GRID_W), :]
            q2 = jnp.concatenate([jnp.where(low, q, jnp.zeros_like(q)),
                                  jnp.where(low, jnp.zeros_like(q), q)], axis=0)
            kw = window_keys(KV_KA + p, w0)
            s = lax.dot_general(q2, kw, _NT, preferred_element_type=f32)
            for g, gs, blocks in groups:
                m = None
                for j in blocks:
                    slot = (j // 2) * KEY_GROUP - off + (NA_WIN_H - 1)
                    blk = _slab(s[gs], j) + bias_ref[p, j % 2, slot, gs, :]
                    s_ref[p, j, gs, :] = blk
                    m = blk if m is None else jnp.maximum(m, blk)
                m_ref[p, gs, :] = jnp.broadcast_to(jnp.max(m, axis=-1, keepdims=True),
                                                   (NA_GROUP, LANES))

    def na_probs(s_ref, m_ref, p_ref):
        for p in range(NA_PAIRS):
            for g, gs, blocks in groups:
                m = m_ref[p, gs, :]
                for j in blocks:
                    p_ref[p, j, gs, :] = jnp.exp2(s_ref[p, j, gs, :] - m).astype(bf16)

    def na_output(r, p_ref):
        w0, _ = window(r)
        q0 = pl.multiple_of(r * GRID_W, GRID_W)
        for p in range(NA_PAIRS):
            vw = jnp.concatenate([window_keys(KV_VA + p, w0), ones], axis=1)
            pm = jnp.concatenate([p_ref[p, j] for j in range(n_blk)], axis=1)
            o2 = jnp.dot(pm, vw, preferred_element_type=f32)
            num = jnp.where(low, o2[0:GRID_W, 0:LANES], o2[GRID_W:, 0:LANES])
            den = jnp.where(low, o2[0:GRID_W, LANES:], o2[GRID_W:, LANES:])
            o = num / den
            gate = tok_ref[TOK_GA + p, pl.ds(q0, GRID_W), :]
            mix_ref[p, pl.ds(q0, GRID_W), :] = (o.astype(bf16) * gate)

    s_refs, m_refs, p_refs = (s0_ref, s1_ref), (m0_ref, m1_ref), (p0_ref, p1_ref)
    for t in range(n_rows + 2):
        if t < n_rows:
            na_scores(t, s_refs[t % 2], m_refs[t % 2])
        if 1 <= t <= n_rows:
            na_probs(s_refs[(t - 1) % 2], m_refs[(t - 1) % 2], p_refs[(t - 1) % 2])
        if t >= 2:
            na_output(t - 2, p_refs[t % 2])

    C = RET_CHUNK
    for c in range(tt // C):
        rws = slice(c * C, (c + 1) * C)
        for h in range(RET_HEADS):
            q = tok_ref[TOK_QR + h, rws, :]
            k = tok_ref[TOK_KR + h, rws, :]
            v = tok_ref[TOK_VR + h, rws, :]
            s = lax.dot_general(q, k, _NT, preferred_element_type=f32) * dmat_ref[h]
            o = jnp.dot(s.astype(bf16), v, preferred_element_type=f32)
            sf = state_ref[h]
            o = o + jnp.dot(q, sf.astype(bf16), preferred_element_type=f32) * dec_ref[DEC_QF, h]
            o = o + jnp.dot(q, sb_ref[c, h], preferred_element_type=f32) * dec_ref[DEC_QB, h]
            kd = (k.astype(f32) * dec_ref[DEC_KF, h]).astype(bf16)
            state_ref[h] = sf * cdec_ref[0, h, 0:1, :] + lax.dot_general(
                kd, v, _TN, preferred_element_type=f32)
            ms = jnp.mean(o * o, axis=-1, keepdims=True)
            rn = o * lax.rsqrt(ms + NORM_EPS) * og_ref[h]
            g = tok_ref[TOK_GR + h, rws, :]
            mix_ref[NA_PAIRS + h, rws, :] = (rn.astype(bf16) * g)

    mix = jnp.concatenate([mix_ref[n] for n in range(NA_PAIRS + RET_HEADS)], axis=1)
    out = jnp.dot(mix, wo_ref[...], preferred_element_type=f32)
    y_ref[...] = x_ref[...] + mod_ref[2:3, :] * out


def _mix_out(x, mod3, tok_arr, kv, sb, bias, dmat, dec, cdec, og, w_out):
    B, T, _ = x.shape
    tt = MIX_TILE
    nt = T // tt
    C = RET_CHUNK
    rows = T // GRID_W
    win = NA_WIN_H * GRID_W
    assert T % tt == 0 and T >= tt + 2 * HALO_TOKENS, (T, tt)
    tok = lambda b, i: (b, i, 0)
    c2 = lambda b, i: (0, 0)
    c3 = lambda b, i: (0, 0, 0)
    c4 = lambda b, i: (0, 0, 0, 0)
    halo_spec = pl.BlockSpec(
        (None, pl.Element(KV_SLABS), pl.Element(tt + 2 * HALO_TOKENS), pl.Element(LANES)),
        lambda b, i: (b, 0, _halo_start(i, tt, T), 0))
    score_shape = (NA_PAIRS, win // LANES, LANES, LANES)
    return pl.pallas_call(
        functools.partial(_mix_out_kernel, rows=rows),
        grid=(B, nt),
        in_specs=[pl.BlockSpec((None, tt, D_MODEL), tok),
                  pl.BlockSpec((None, 3, D_MODEL), lambda b, i: (b, 0, 0)),
                  pl.BlockSpec((None, TOK_SLABS, tt, LANES), lambda b, i: (b, 0, i, 0)),
                  halo_spec,
                  pl.BlockSpec((NA_PAIRS, 2, BIAS_SLOTS, LANES, LANES),
                               lambda b, i: (0, 0, 0, 0, 0), pipeline_mode=pl.Buffered(1)),
                  pl.BlockSpec((None, tt // C, RET_HEADS, C, C), lambda b, i: (b, i, 0, 0, 0)),
                  pl.BlockSpec((RET_HEADS, C, C), c3),
                  pl.BlockSpec((DEC_TABLES, RET_HEADS, C, LANES), c4),
                  pl.BlockSpec((2, RET_HEADS, 8, LANES), c4),
                  pl.BlockSpec((RET_HEADS, 1, RET_HEAD_DIM), c3),
                  pl.BlockSpec((D_MODEL, D_MODEL), c2, pipeline_mode=pl.Buffered(1))],
        out_specs=pl.BlockSpec((None, tt, D_MODEL), tok),
        out_shape=jax.ShapeDtypeStruct((B, T, D_MODEL), jnp.float32),
        scratch_shapes=[pltpu.VMEM((RET_HEADS, C, C), jnp.float32),
                        pltpu.VMEM((NA_PAIRS + RET_HEADS, tt, LANES), jnp.bfloat16),
                        pltpu.VMEM(score_shape, jnp.float32),
                        pltpu.VMEM(score_shape, jnp.float32),
                        pltpu.VMEM((NA_PAIRS, LANES, LANES), jnp.float32),
                        pltpu.VMEM((NA_PAIRS, LANES, LANES), jnp.float32),
                        pltpu.VMEM(score_shape, jnp.bfloat16),
                        pltpu.VMEM(score_shape, jnp.bfloat16)],
        compiler_params=pltpu.CompilerParams(dimension_semantics=("arbitrary", "arbitrary"),
                                             vmem_limit_bytes=VMEM_LIMIT_BYTES),
        name="mix_out",
    )(x, mod3, tok_arr, kv, bias, sb, dmat, dec, cdec, og, w_out)


def _rope_tables(positions):
    half = RET_HEAD_DIM // 2
    inv = ROPE_BASE ** (-jnp.arange(half, dtype=jnp.float32) / half)
    ang = positions.astype(jnp.float32)[:, None] * inv[None, :]
    cos, sin = jnp.cos(ang), jnp.sin(ang)
    return jnp.concatenate([cos, cos], axis=1), jnp.concatenate([sin, sin], axis=1)


def _layer(x, mod3, shared):
    (gain, w_in, bd, qg, kg, cosw, sinw, base, bias, dmat, dec, cdec, og, w_out) = shared
    nt = x.shape[1] // PROJ_TILE
    tok_arr, kv, sb = _in_proj(x, mod3, gain, w_in, bd, qg, kg, cosw, sinw, base[:nt], dec, cdec)
    return _mix_out(x, mod3, tok_arr, kv, sb, bias, dmat, dec, cdec, og, w_out)


def kernel(x_prompt, x_sample, c_prompt, c_sample, norm_gain, w_ada, b_ada, w_in, na_q_gain,
           na_k_gain, na_rpb, ret_decay_f, ret_decay_b, ret_out_gain, w_out):
    depth = norm_gain.shape[0]
    bp, bs = x_prompt.shape[0], x_sample.shape[0]
    nb = -(-(bp + bs) // 8) * 8
    t_max = max(x_prompt.shape[1], x_sample.shape[1])
    cosw, sinw = _rope_tables(jnp.arange(PROJ_TILE))
    base = jnp.stack(_rope_tables(jnp.arange(t_max // PROJ_TILE) * PROJ_TILE), axis=1)
    hid = np.arange(256) // NA_HEAD_DIM
    bd = jnp.asarray(hid[:, None] == hid[None, :], jnp.bfloat16)
    c_all = jnp.concatenate(
        [c_prompt, c_sample, jnp.zeros((nb - bp - bs, D_MODEL), jnp.float32)], axis=0)
    y_prompt, y_sample = x_prompt, x_sample
    for l in range(depth):
        rpb = jnp.pad(na_rpb[l], ((0, 0), (0, 0), (0, LANES - (2 * NA_WIN_W - 1))))
        mod, dmat, dec, cdec, bias = _prep(c_all, w_ada[l], b_ada[l][None],
                                           ret_decay_f[l], ret_decay_b[l], rpb)
        mod3 = mod.reshape(nb, 3, D_MODEL)
        w_seg = w_in[l].astype(jnp.bfloat16).reshape(D_MODEL, IN_WIDTH // SEG, SEG)
        shared = (norm_gain[l][None], w_seg.transpose(1, 0, 2), bd,
                  jnp.tile(na_q_gain[l], NA_HEADS)[None], jnp.tile(na_k_gain[l], NA_HEADS)[None],
                  cosw, sinw, base, bias, dmat, dec, cdec,
                  ret_out_gain[l].reshape(RET_HEADS, 1, RET_HEAD_DIM),
                  w_out[l].astype(jnp.bfloat16))
        y_prompt = _layer(y_prompt, mod3[:bp], shared)
        y_sample = _layer(y_sample, mod3[bp:bp + bs], shared)
    return (y_prompt, y_sample)
```

```python
import functools

import numpy as np
import jax
import jax.numpy as jnp
from jax import lax
from jax.experimental import pallas as pl
from jax.experimental.pallas import tpu as pltpu

D_MODEL = 1024
GRID_W = 64
NA_HEADS = 8
NA_HEAD_DIM = 64
NA_WIDTH = NA_HEADS * NA_HEAD_DIM
NA_PAIRS = NA_HEADS // 2
NA_WIN_H = 8
NA_WIN_W = 16
RET_HEADS = 4
RET_HEAD_DIM = 128
RET_WIDTH = RET_HEADS * RET_HEAD_DIM
RET_CHUNK = 128
ROPE_BASE = 10000.0
NORM_EPS = 1e-6
LOG2E = 1.4426950408889634
IN_WIDTH = 4 * NA_WIDTH + 4 * RET_WIDTH
SEG = 512
LANES = 128

PROJ_TILE = 1024
MIX_TILE = 1024
HALO_ROWS = NA_WIN_H // 2
HALO_TOKENS = HALO_ROWS * GRID_W
HALF_W = GRID_W // 2
KEY_GROUP = LANES // HALF_W
BIAS_SLOTS = 2 * NA_WIN_H - KEY_GROUP
NA_GROUP = 16

TOK_QA, TOK_GA, TOK_QR, TOK_KR, TOK_VR, TOK_GR = 0, 4, 8, 12, 16, 20
TOK_SLABS = 24
KV_KA, KV_VA = 0, NA_PAIRS
KV_SLABS = 2 * NA_PAIRS
DEC_QF, DEC_QB, DEC_KF, DEC_KB = range(4)
DEC_TABLES = 4
VMEM_LIMIT_BYTES = 56 * 1024 * 1024

_NT = (((1,), (1,)), ((), ()))
_TN = (((0,), (0,)), ((), ()))


def _silu(v):
    return v / (1.0 + jnp.exp(-v))


def _slab(v, n):
    return v[:, n * LANES:(n + 1) * LANES]


def _prep_kernel(dec_f_ref, dec_b_ref, c_ref, w_ref, b_ref, rpb_ref,
                 mod_ref, dmat_ref, dec_ref, cdec_ref, bias_ref):
    c = c_ref[...]
    mod_ref[...] = jnp.dot(_silu(c), w_ref[...], preferred_element_type=jnp.float32) + b_ref[...]

    @pl.when(pl.program_id(0) == 0)
    def _():
        C = RET_CHUNK
        pos = lax.broadcasted_iota(jnp.int32, (C, LANES), 0).astype(jnp.float32)
        ri = lax.broadcasted_iota(jnp.int32, (C, C), 0)
        ci = lax.broadcasted_iota(jnp.int32, (C, C), 1)
        diff = (ri - ci).astype(jnp.float32)
        for h in range(RET_HEADS):
            lgf = -jnp.exp(jnp.full((1, LANES), dec_f_ref[h], jnp.float32))
            lgb = -jnp.exp(jnp.full((1, LANES), dec_b_ref[h], jnp.float32))
            dec_ref[DEC_QF, h] = jnp.exp(lgf * (pos + 1.0))
            dec_ref[DEC_QB, h] = jnp.exp(lgb * (C - pos))
            dec_ref[DEC_KF, h] = jnp.exp(lgf * (C - 1.0 - pos))
            dec_ref[DEC_KB, h] = jnp.exp(lgb * pos)
            cdec_ref[0, h] = jnp.broadcast_to(jnp.exp(lgf * C), (8, LANES))
            cdec_ref[1, h] = jnp.broadcast_to(jnp.exp(lgb * C), (8, LANES))
            dmat_ref[h] = jnp.where(diff >= 0, jnp.exp(lgf * jnp.maximum(diff, 0.0)),
                                    jnp.exp(lgb * jnp.maximum(-diff, 0.0)))

        qcol = lax.broadcasted_iota(jnp.int32, (GRID_W, LANES), 0)
        lane = lax.broadcasted_iota(jnp.int32, (GRID_W, LANES), 1)
        wstart = jnp.clip(qcol - NA_WIN_W // 2, 0, GRID_W - NA_WIN_W)
        for h in range(NA_HEADS):
            for half in range(2):
                kcol = half * HALF_W + lane % HALF_W
                valid = (kcol >= wstart) & (kcol < wstart + NA_WIN_W)
                for d in range(BIAS_SLOTS):
                    blk = None
                    for kr in range(KEY_GROUP):
                        shift = (kr * HALF_W - half * HALF_W - (NA_WIN_W - 1)) % LANES
                        rolled = pltpu.roll(
                            jnp.broadcast_to(rpb_ref[h, d + kr:d + kr + 1, :], (GRID_W, LANES)),
                            shift, 1, stride=1, stride_axis=0)
                        blk = rolled if blk is None else jnp.where(lane >= kr * HALF_W, rolled, blk)
                    bias_ref[h // 2, half, d, (h % 2) * GRID_W:(h % 2 + 1) * GRID_W, :] = (
                        jnp.where(valid, blk * LOG2E, -jnp.inf))


def _prep(c_all, w_ada, b_ada, dec_f, dec_b, rpb):
    nb = c_all.shape[0]
    C = RET_CHUNK
    f32 = jnp.float32
    smem = pl.BlockSpec(memory_space=pltpu.SMEM)
    const2 = lambda j: (0, 0)
    const3 = lambda j: (0, 0, 0)
    const4 = lambda j: (0, 0, 0, 0)
    return pl.pallas_call(
        _prep_kernel,
        grid=(3,),
        in_specs=[smem, smem,
                  pl.BlockSpec((nb, D_MODEL), const2),
                  pl.BlockSpec((D_MODEL, D_MODEL), lambda j: (0, j)),
                  pl.BlockSpec((1, D_MODEL), lambda j: (0, j)),
                  pl.BlockSpec((NA_HEADS, 2 * NA_WIN_H - 1, LANES), const3)],
        out_specs=[pl.BlockSpec((nb, D_MODEL), lambda j: (0, j)),
                   pl.BlockSpec((RET_HEADS, C, C), const3),
                   pl.BlockSpec((DEC_TABLES, RET_HEADS, C, LANES), const4),
                   pl.BlockSpec((2, RET_HEADS, 8, LANES), const4),
                   pl.BlockSpec((NA_PAIRS, 2, BIAS_SLOTS, LANES, LANES),
                                lambda j: (0, 0, 0, 0, 0))],
        out_shape=[jax.ShapeDtypeStruct((nb, 3 * D_MODEL), f32),
                   jax.ShapeDtypeStruct((RET_HEADS, C, C), f32),
                   jax.ShapeDtypeStruct((DEC_TABLES, RET_HEADS, C, LANES), f32),
                   jax.ShapeDtypeStruct((2, RET_HEADS, 8, LANES), f32),
                   jax.ShapeDtypeStruct((NA_PAIRS, 2, BIAS_SLOTS, LANES, LANES), f32)],
        compiler_params=pltpu.CompilerParams(dimension_semantics=("arbitrary",),
                                             vmem_limit_bytes=VMEM_LIMIT_BYTES),
        name="prep",
    )(dec_f, dec_b, c_all, w_ada, b_ada, rpb)


def _in_proj_kernel(x_ref, mod_ref, gain_ref, w_ref, bd_ref, qg_ref, kg_ref, cos_ref, sin_ref,
                    base_ref, dec_ref, cdec_ref,
                    tok_ref, kv_ref, sb_ref,
                    state_ref):
    bf16, f32 = jnp.bfloat16, jnp.float32
    tt = x_ref.shape[0]

    @pl.when(pl.program_id(1) == 0)
    def _():
        state_ref[...] = jnp.zeros_like(state_ref)

    x = x_ref[...]
    ms = jnp.mean(x * x, axis=-1, keepdims=True)
    a = gain_ref[...] * (1.0 + mod_ref[1:2, :])
    hb = (x * lax.rsqrt(ms + NORM_EPS) * a + mod_ref[0:1, :]).astype(bf16)

    def seg(s):
        return jnp.dot(hb, w_ref[s], preferred_element_type=f32)

    def head_norm(p, g):
        pp = (p * p).astype(bf16)
        ss = jnp.concatenate(
            [jnp.dot(pp[:, c * 256:(c + 1) * 256], bd_ref[...], preferred_element_type=f32)
             for c in range(NA_WIDTH // 256)], axis=1)
        return p * lax.rsqrt(ss * (1.0 / NA_HEAD_DIM) + NORM_EPS) * g

    def put(ref, first, val):
        for n in range(val.shape[1] // LANES):
            ref[first + n] = _slab(val, n)

    put(tok_ref, TOK_QA,
        head_norm(seg(0), qg_ref[...] * (NA_HEAD_DIM ** -0.5 * LOG2E)).astype(bf16))
    put(kv_ref, KV_KA, head_norm(seg(1), kg_ref[...]).astype(bf16))
    put(kv_ref, KV_VA, seg(2).astype(bf16))
    put(tok_ref, TOK_GA, _silu(seg(3)).astype(bf16))

    c0, s0 = base_ref[0:1, :], base_ref[1:2, :]
    cw, sw = cos_ref[...], sin_ref[...]
    hlane = lax.broadcasted_iota(jnp.int32, (1, RET_HEAD_DIM), 1)
    cosf = c0 * cw - s0 * sw
    sinf = (s0 * cw + c0 * sw) * jnp.where(hlane < RET_HEAD_DIM // 2, -1.0, 1.0)

    def rotary(p):
        outs = []
        for h in range(RET_HEADS):
            ph = _slab(p, h)
            outs.append(ph * cosf + pltpu.roll(ph, RET_HEAD_DIM // 2, 1) * sinf)
        return jnp.concatenate(outs, axis=1)

    put(tok_ref, TOK_QR, rotary(seg(4)).astype(bf16))
    kr = rotary(seg(5)) * (RET_HEAD_DIM ** -0.5)
    put(tok_ref, TOK_KR, kr.astype(bf16))
    vb = seg(6).astype(bf16)
    put(tok_ref, TOK_VR, vb)
    put(tok_ref, TOK_GR, _silu(seg(7)).astype(bf16))

    C = RET_CHUNK
    for c in reversed(range(tt // C)):
        rows = slice(c * C, (c + 1) * C)
        for h in range(RET_HEADS):
            s_old = state_ref[h]
            sb_ref[c, h] = s_old.astype(bf16)
            kd = (_slab(kr[rows], h) * dec_ref[DEC_KB, h]).astype(bf16)
            kv = lax.dot_general(kd, _slab(vb[rows], h), _TN, preferred_element_type=f32)
            state_ref[h] = s_old * cdec_ref[1, h, 0:1, :] + kv


def _in_proj(x, mod3, gain, w_in, bd, qg, kg, cosw, sinw, base, dec, cdec):
    B, T, _ = x.shape
    tt = PROJ_TILE
    nt = T // tt
    C = RET_CHUNK
    rev = lambda b, i: (b, nt - 1 - i, 0)
    rev4 = lambda b, i: (b, 0, nt - 1 - i, 0)
    c2 = lambda b, i: (0, 0)
    c3 = lambda b, i: (0, 0, 0)
    c4 = lambda b, i: (0, 0, 0, 0)
    return pl.pallas_call(
        _in_proj_kernel,
        grid=(B, nt),
        in_specs=[pl.BlockSpec((None, tt, D_MODEL), rev),
                  pl.BlockSpec((None, 3, D_MODEL), lambda b, i: (b, 0, 0)),
                  pl.BlockSpec((1, D_MODEL), c2),
                  pl.BlockSpec((IN_WIDTH // SEG, D_MODEL, SEG), c3, pipeline_mode=pl.Buffered(1)),
                  pl.BlockSpec((256, 256), c2),
                  pl.BlockSpec((1, NA_WIDTH), c2),
                  pl.BlockSpec((1, NA_WIDTH), c2),
                  pl.BlockSpec((tt, RET_HEAD_DIM), c2),
                  pl.BlockSpec((tt, RET_HEAD_DIM), c2),
                  pl.BlockSpec((None, 2, RET_HEAD_DIM), lambda b, i: (nt - 1 - i, 0, 0)),
                  pl.BlockSpec((DEC_TABLES, RET_HEADS, C, LANES), c4),
                  pl.BlockSpec((2, RET_HEADS, 8, LANES), c4)],
        out_specs=[pl.BlockSpec((None, TOK_SLABS, tt, LANES), rev4),
                   pl.BlockSpec((None, KV_SLABS, tt, LANES), rev4),
                   pl.BlockSpec((None, tt // C, RET_HEADS, C, C),
                                lambda b, i: (b, nt - 1 - i, 0, 0, 0))],
        out_shape=[jax.ShapeDtypeStruct((B, TOK_SLABS, T, LANES), jnp.bfloat16),
                   jax.ShapeDtypeStruct((B, KV_SLABS, T, LANES), jnp.bfloat16),
                   jax.ShapeDtypeStruct((B, T // C, RET_HEADS, C, C), jnp.bfloat16)],
        scratch_shapes=[pltpu.VMEM((RET_HEADS, C, C), jnp.float32)],
        compiler_params=pltpu.CompilerParams(dimension_semantics=("arbitrary", "arbitrary"),
                                             vmem_limit_bytes=VMEM_LIMIT_BYTES),
        name="in_proj",
    )(x, mod3, gain, w_in, bd, qg, kg, cosw, sinw, base, dec, cdec)


def _halo_start(i, tt, seq_len):
    units = jnp.clip(i * (tt // HALO_TOKENS) - 1, 0, (seq_len - tt) // HALO_TOKENS - 2)
    return units * HALO_TOKENS


def _mix_out_kernel(x_ref, mod_ref, tok_ref, kv_ref, bias_ref, sb_ref,
                    dmat_ref, dec_ref, cdec_ref, og_ref, wo_ref,
                    y_ref,
                    state_ref, mix_ref, s0_ref, s1_ref, m0_ref, m1_ref,
                    p0_ref, p1_ref, *, rows):
    bf16, f32 = jnp.bfloat16, jnp.float32
    tt = x_ref.shape[0]
    n_rows = tt // GRID_W
    i = pl.program_id(1)

    @pl.when(i == 0)
    def _():
        state_ref[...] = jnp.zeros_like(state_ref)

    first_row = _halo_start(i, tt, rows * GRID_W) // GRID_W

    lane = lax.broadcasted_iota(jnp.int32, (GRID_W, LANES), 1)
    low = lane < NA_HEAD_DIM
    win = NA_WIN_H * GRID_W
    n_blk = win // LANES
    ones = jnp.ones((win, LANES), bf16)

    def window(r):
        grow = i * n_rows + r
        start = jnp.clip(grow - NA_WIN_H // 2, 0, rows - NA_WIN_H)
        w0 = pl.multiple_of((start - first_row) * GRID_W, GRID_W)
        return w0, grow - start

    def window_keys(slab, w0):
        parts = []
        for j in range(n_blk):
            for kr in range(KEY_GROUP):
                start = w0 + ((j // 2) * KEY_GROUP + kr) * GRID_W + (j % 2) * HALF_W
                parts.append(kv_ref[slab, pl.ds(pl.multiple_of(start, HALF_W), HALF_W), :])
        return jnp.concatenate(parts, axis=0)

    def halves_needed(g):
        q_lo = (g * NA_GROUP) % GRID_W
        q_hi = q_lo + NA_GROUP - 1
        c_lo = min(max(q_lo - NA_WIN_W // 2, 0), GRID_W - NA_WIN_W)
        c_hi = min(max(q_hi - NA_WIN_W // 2, 0), GRID_W - NA_WIN_W) + NA_WIN_W - 1
        return [half for half in range(2)
                if c_lo < (half + 1) * HALF_W and c_hi >= half * HALF_W]

    groups = [(g, slice(g * NA_GROUP, (g + 1) * NA_GROUP),
               [j for j in range(n_blk) if j % 2 in halves_needed(g)])
              for g in range(LANES // NA_GROUP)]

    @pl.when((pl.program_id(0) == 0) & (i == 0))
    def _():
        p0_ref[...] = jnp.zeros_like(p0_ref)
        p1_ref[...] = jnp.zeros_like(p1_ref)

    def na_scores(r, s_ref, m_ref):
        w0, off = window(r)
        q0 = pl.multiple_of(r * GRID_W, GRID_W)
        for p in range(NA_PAIRS):
            q = tok_ref[TOK_QA + p, pl.ds(q0, GRID_W), :]
            q2 = jnp.concatenate([jnp.where(low, q, jnp.zeros_like(q)),
                                  jnp.where(low, jnp.zeros_like(q), q)], axis=0)
            kw = window_keys(KV_KA + p, w0)
            s = lax.dot_general(q2, kw, _NT, preferred_element_type=f32)
            for g, gs, blocks in groups:
                m = None
                for j in blocks:
                    slot = (j // 2) * KEY_GROUP - off + (NA_WIN_H - 1)
                    blk = _slab(s[gs], j) + bias_ref[p, j % 2, slot, gs, :]
                    s_ref[p, j, gs, :] = blk
                    m = blk if m is None else jnp.maximum(m, blk)
                m_ref[p, gs, :] = jnp.broadcast_to(jnp.max(m, axis=-1, keepdims=True),
                                                   (NA_GROUP, LANES))

    def na_probs(s_ref, m_ref, p_ref):
        for p in range(NA_PAIRS):
            for g, gs, blocks in groups:
                m = m_ref[p, gs, :]
                for j in blocks:
                    p_ref[p, j, gs, :] = jnp.exp2(s_ref[p, j, gs, :] - m).astype(bf16)

    def na_output(r, p_ref):
        w0, _ = window(r)
        q0 = pl.multiple_of(r * GRID_W, GRID_W)
        for p in range(NA_PAIRS):
            vw = jnp.concatenate([window_keys(KV_VA + p, w0), ones], axis=1)
            pm = jnp.concatenate([p_ref[p, j] for j in range(n_blk)], axis=1)
            o2 = jnp.dot(pm, vw, preferred_element_type=f32)
            num = jnp.where(low, o2[0:GRID_W, 0:LANES], o2[GRID_W:, 0:LANES])
            den = jnp.where(low, o2[0:GRID_W, LANES:], o2[GRID_W:, LANES:])
            o = num / den
            gate = tok_ref[TOK_GA + p, pl.ds(q0, GRID_W), :]
            mix_ref[p, pl.ds(q0, GRID_W), :] = (o.astype(bf16) * gate)

    s_refs, m_refs, p_refs = (s0_ref, s1_ref), (m0_ref, m1_ref), (p0_ref, p1_ref)
    C = RET_CHUNK

    def ret_chunk(c):
        rws = slice(c * C, (c + 1) * C)
        for h in range(RET_HEADS):
            q = tok_ref[TOK_QR + h, rws, :]
            k = tok_ref[TOK_KR + h, rws, :]
            v = tok_ref[TOK_VR + h, rws, :]
            s = lax.dot_general(q, k, _NT, preferred_element_type=f32) * dmat_ref[h]
            o = jnp.dot(s.astype(bf16), v, preferred_element_type=f32)
            sf = state_ref[h]
            o = o + jnp.dot(q, sf.astype(bf16), preferred_element_type=f32) * dec_ref[DEC_QF, h]
            o = o + jnp.dot(q, sb_ref[c, h], preferred_element_type=f32) * dec_ref[DEC_QB, h]
            kd = (k.astype(f32) * dec_ref[DEC_KF, h]).astype(bf16)
            state_ref[h] = sf * cdec_ref[0, h, 0:1, :] + lax.dot_general(
                kd, v, _TN, preferred_element_type=f32)
            ms = jnp.mean(o * o, axis=-1, keepdims=True)
            rn = o * lax.rsqrt(ms + NORM_EPS) * og_ref[h]
            g = tok_ref[TOK_GR + h, rws, :]
            mix_ref[NA_PAIRS + h, rws, :] = (rn.astype(bf16) * g)

    for t in range(n_rows + 2):
        if t < n_rows:
            na_scores(t, s_refs[t % 2], m_refs[t % 2])
        if 1 <= t <= n_rows:
            na_probs(s_refs[(t - 1) % 2], m_refs[(t - 1) % 2], p_refs[(t - 1) % 2])
        if t >= 2:
            na_output(t - 2, p_refs[t % 2])
        if t % 2 == 1 and t // 2 < tt // C:
            ret_chunk(t // 2)

    mix = jnp.concatenate([mix_ref[n] for n in range(NA_PAIRS + RET_HEADS)], axis=1)
    out = jnp.dot(mix, wo_ref[...], preferred_element_type=f32)
    y_ref[...] = x_ref[...] + mod_ref[2:3, :] * out


def _mix_out(x, mod3, tok_arr, kv, sb, bias, dmat, dec, cdec, og, w_out):
    B, T, _ = x.shape
    tt = MIX_TILE
    nt = T // tt
    C = RET_CHUNK
    rows = T // GRID_W
    win = NA_WIN_H * GRID_W
    assert T % tt == 0 and T >= tt + 2 * HALO_TOKENS, (T, tt)
    tok = lambda b, i: (b, i, 0)
    c2 = lambda b, i: (0, 0)
    c3 = lambda b, i: (0, 0, 0)
    c4 = lambda b, i: (0, 0, 0, 0)
    halo_spec = pl.BlockSpec(
        (None, pl.Element(KV_SLABS), pl.Element(tt + 2 * HALO_TOKENS), pl.Element(LANES)),
        lambda b, i: (b, 0, _halo_start(i, tt, T), 0))
    score_shape = (NA_PAIRS, win // LANES, LANES, LANES)
    return pl.pallas_call(
        functools.partial(_mix_out_kernel, rows=rows),
        grid=(B, nt),
        in_specs=[pl.BlockSpec((None, tt, D_MODEL), tok),
                  pl.BlockSpec((None, 3, D_MODEL), lambda b, i: (b, 0, 0)),
                  pl.BlockSpec((None, TOK_SLABS, tt, LANES), lambda b, i: (b, 0, i, 0)),
                  halo_spec,
                  pl.BlockSpec((NA_PAIRS, 2, BIAS_SLOTS, LANES, LANES),
                               lambda b, i: (0, 0, 0, 0, 0), pipeline_mode=pl.Buffered(1)),
                  pl.BlockSpec((None, tt // C, RET_HEADS, C, C), lambda b, i: (b, i, 0, 0, 0)),
                  pl.BlockSpec((RET_HEADS, C, C), c3),
                  pl.BlockSpec((DEC_TABLES, RET_HEADS, C, LANES), c4),
                  pl.BlockSpec((2, RET_HEADS, 8, LANES), c4),
                  pl.BlockSpec((RET_HEADS, 1, RET_HEAD_DIM), c3),
                  pl.BlockSpec((D_MODEL, D_MODEL), c2, pipeline_mode=pl.Buffered(1))],
        out_specs=pl.BlockSpec((None, tt, D_MODEL), tok),
        out_shape=jax.ShapeDtypeStruct((B, T, D_MODEL), jnp.float32),
        scratch_shapes=[pltpu.VMEM((RET_HEADS, C, C), jnp.float32),
                        pltpu.VMEM((NA_PAIRS + RET_HEADS, tt, LANES), jnp.bfloat16),
                        pltpu.VMEM(score_shape, jnp.float32),
                        pltpu.VMEM(score_shape, jnp.float32),
                        pltpu.VMEM((NA_PAIRS, LANES, LANES), jnp.float32),
                        pltpu.VMEM((NA_PAIRS, LANES, LANES), jnp.float32),
                        pltpu.VMEM(score_shape, jnp.bfloat16),
                        pltpu.VMEM(score_shape, jnp.bfloat16)],
        compiler_params=pltpu.CompilerParams(dimension_semantics=("arbitrary", "arbitrary"),
                                             vmem_limit_bytes=VMEM_LIMIT_BYTES),
        name="mix_out",
    )(x, mod3, tok_arr, kv, bias, sb, dmat, dec, cdec, og, w_out)


def _rope_tables(positions):
    half = RET_HEAD_DIM // 2
    inv = ROPE_BASE ** (-jnp.arange(half, dtype=jnp.float32) / half)
    ang = positions.astype(jnp.float32)[:, None] * inv[None, :]
    cos, sin = jnp.cos(ang), jnp.sin(ang)
    return jnp.concatenate([cos, cos], axis=1), jnp.concatenate([sin, sin], axis=1)


def _layer(x, mod3, shared):
    (gain, w_in, bd, qg, kg, cosw, sinw, base, bias, dmat, dec, cdec, og, w_out) = shared
    nt = x.shape[1] // PROJ_TILE
    tok_arr, kv, sb = _in_proj(x, mod3, gain, w_in, bd, qg, kg, cosw, sinw, base[:nt], dec, cdec)
    return _mix_out(x, mod3, tok_arr, kv, sb, bias, dmat, dec, cdec, og, w_out)


def kernel(x_prompt, x_sample, c_prompt, c_sample, norm_gain, w_ada, b_ada, w_in, na_q_gain,
           na_k_gain, na_rpb, ret_decay_f, ret_decay_b, ret_out_gain, w_out):
    depth = norm_gain.shape[0]
    bp, bs = x_prompt.shape[0], x_sample.shape[0]
    nb = -(-(bp + bs) // 8) * 8
    t_max = max(x_prompt.shape[1], x_sample.shape[1])
    cosw, sinw = _rope_tables(jnp.arange(PROJ_TILE))
    base = jnp.stack(_rope_tables(jnp.arange(t_max // PROJ_TILE) * PROJ_TILE), axis=1)
    hid = np.arange(256) // NA_HEAD_DIM
    bd = jnp.asarray(hid[:, None] == hid[None, :], jnp.bfloat16)
    c_all = jnp.concatenate(
        [c_prompt, c_sample, jnp.zeros((nb - bp - bs, D_MODEL), jnp.float32)], axis=0)
    y_prompt, y_sample = x_prompt, x_sample
    for l in range(depth):
        rpb = jnp.pad(na_rpb[l], ((0, 0), (0, 0), (0, LANES - (2 * NA_WIN_W - 1))))
        mod, dmat, dec, cdec, bias = _prep(c_all, w_ada[l], b_ada[l][None],
                                           ret_decay_f[l], ret_decay_b[l], rpb)
        mod3 = mod.reshape(nb, 3, D_MODEL)
        w_seg = w_in[l].astype(jnp.bfloat16).reshape(D_MODEL, IN_WIDTH // SEG, SEG)
        shared = (norm_gain[l][None], w_seg.transpose(1, 0, 2), bd,
                  jnp.tile(na_q_gain[l], NA_HEADS)[None], jnp.tile(na_k_gain[l], NA_HEADS)[None],
                  cosw, sinw, base, bias, dmat, dec, cdec,
                  ret_out_gain[l].reshape(RET_HEADS, 1, RET_HEAD_DIM),
                  w_out[l].astype(jnp.bfloat16))
        y_prompt = _layer(y_prompt, mod3[:bp], shared)
        y_sample = _layer(y_sample, mod3[bp:bp + bs], shared)
    return (y_prompt, y_sample)
```
